```python
import jax, jax.numpy as jnp
from jax import lax
import numpy as np

D_MODEL = 4096
BATCH = 2
SEQ = 4096
DEPTH = 4
DEC_BATCH = 8
DEC_SEQ = 64
PAST_LEN = 4096

CHUNK = 64
N_MIXERS = 3
N_POOL = (DEPTH + 2) // 3
N_SGU = (DEPTH + 1) // 3
N_SB = DEPTH // 3
POOL_WINDOWS = (2, 4, 8, 16)
POOL_GROUPS = 4
POOL_GW = D_MODEL // POOL_GROUPS
POOL_HIST = 15
SGU_WIDTH = D_MODEL
SGU_BLOCK = 128
SGU_GROUPS = 8
SGU_GW = SGU_WIDTH // SGU_GROUPS
SB_HEADS = 32
SB_HEAD_DIM = D_MODEL // SB_HEADS
SB_QBLOCK = 128
D_FF = 4 * D_MODEL
NORM_EPS = 1e-6

kernel_name = 'hybrid_pool_sgu_stickbreak_stream_step'


def rms_norm(x, g):
    xf = x.astype(jnp.float32)
    y = xf * lax.rsqrt(jnp.mean(xf * xf, axis=-1, keepdims=True) + NORM_EPS)
    return (y * g.astype(jnp.float32)).astype(x.dtype)


def sq_relu_mlp(h, w_up, w_down):
    a = jax.nn.relu(h @ w_up)
    return (a * a) @ w_down


def pool_mixer(h, hist, hist_valid, w_grp, scale):
    B, T, _ = h.shape
    buf = jnp.concatenate([hist.astype(h.dtype), h], axis=1)
    csum = jnp.pad(jnp.cumsum(buf.astype(jnp.float32), axis=1), ((0, 0), (1, 0), (0, 0)))
    ind = jnp.concatenate([hist_valid, jnp.ones((T,), jnp.float32)])
    ccnt = jnp.pad(jnp.cumsum(ind), (1, 0))
    end = POOL_HIST + 1
    means = []
    for g, w in enumerate(POOL_WINDOWS):
        cs = csum[:, :, g * POOL_GW:(g + 1) * POOL_GW]
        s = cs[:, end:end + T] - cs[:, end - w:end - w + T]
        c = ccnt[end:end + T] - ccnt[end - w:end - w + T]
        means.append(s / c[None, :, None])
    mean = jnp.stack(means, axis=2)
    d = mean - h.reshape(B, T, POOL_GROUPS, POOL_GW).astype(jnp.float32)
    y = jnp.einsum('btgc,gcd->btgd', d.astype(h.dtype), w_grp).reshape(B, T, D_MODEL)
    return y * scale, buf[:, -POOL_HIST:]


def sgu_mixer(h, w_in, b_in, g_v, w_s, b_s, w_out):
    B, T, _ = h.shape
    z = jax.nn.gelu(h @ w_in + b_in, approximate=False)
    u, v = jnp.split(z, 2, axis=-1)
    v = rms_norm(v, g_v)
    blk = min(T, SGU_BLOCK)
    nb = T // blk
    pos = jnp.arange(blk)
    mask = (pos[None, :] // CHUNK) <= (pos[:, None] // CHUNK)
    ws = jnp.where(mask[None], w_s[:, :blk, :blk], 0)
    vb = v.reshape(B, nb, blk, SGU_GROUPS, SGU_GW)
    f = jnp.einsum('gij,bnjgc->bnigc', ws, vb) + b_s[:, :blk].T[None, None, :, :, None]
    y = (u * f.reshape(B, T, SGU_WIDTH)) @ w_out
    return y, v


def sb_attend(q, k, v, q_pos, k_pos):
    z = jnp.einsum('bqhd,bkhd->bhqk', q, k).astype(jnp.float32) * (SB_HEAD_DIM ** -0.5)
    allowed = k_pos[None, :] < q_pos[:, None]
    ls = jax.nn.log_sigmoid(z)
    l_not = jnp.where(allowed, ls - z, 0.0)
    tail = lax.cumsum(l_not, axis=3, reverse=True) - l_not
    a = jnp.where(allowed, jnp.exp(ls + tail), 0.0)
    return jnp.einsum('bhqk,bkhd->bqhd', a.astype(v.dtype), v)


def sb_project(h, w_qkv):
    B, T, _ = h.shape
    qkv = (h @ w_qkv).reshape(B, T, 3, SB_HEADS, SB_HEAD_DIM)
    return qkv[:, :, 0], qkv[:, :, 1], qkv[:, :, 2]


def sb_prompt(h, w_qkv, w_o):
    B, T, _ = h.shape
    q, k, v = sb_project(h, w_qkv)
    pos = jnp.arange(T)
    outs = []
    for b in range(T // SB_QBLOCK):
        lo, hi = b * SB_QBLOCK, (b + 1) * SB_QBLOCK
        outs.append(sb_attend(q[:, lo:hi], k[:, :hi], v[:, :hi], pos[lo:hi], pos[:hi]))
    o = jnp.concatenate(outs, axis=1).reshape(B, T, D_MODEL)
    return o @ w_o, k, v


def sb_sample(h, ck, cv, w_qkv, w_o):
    B, T, _ = h.shape
    P = ck.shape[1]
    q, k, v = sb_project(h, w_qkv)
    k_all = jnp.concatenate([ck.astype(k.dtype), k], axis=1)
    v_all = jnp.concatenate([cv.astype(v.dtype), v], axis=1)
    o = sb_attend(q, k_all, v_all, P + jnp.arange(T), jnp.arange(P + T))
    return o.reshape(B, T, D_MODEL) @ w_o, k, v


def setup_inputs(seed: int = 0) -> dict:
    key = jax.random.key(seed)
    ks = jax.random.split(key, 24)

    def nrm(k, shape, s):
        return jax.random.normal(k, shape, jnp.float32) * s

    def gain(k, shape):
        return 1.0 + 0.05 * jax.random.normal(k, shape, jnp.float32)

    return {
        'x_prompt': nrm(ks[0], (BATCH, SEQ, D_MODEL), 1.0),
        'x_sample': nrm(ks[1], (DEC_BATCH, DEC_SEQ, D_MODEL), 1.0),
        'cache_pool': nrm(ks[2], (N_POOL, DEC_BATCH, POOL_HIST, D_MODEL), 1.0),
        'cache_k': nrm(ks[3], (N_SB, DEC_BATCH, PAST_LEN, SB_HEADS, SB_HEAD_DIM), 1.0),
        'cache_v': nrm(ks[4], (N_SB, DEC_BATCH, PAST_LEN, SB_HEADS, SB_HEAD_DIM), 1.0),
        'g_mix_pre': gain(ks[5], (DEPTH, D_MODEL)),
        'g_mix_post': gain(ks[6], (DEPTH, D_MODEL)),
        'g_ffn_pre': gain(ks[7], (DEPTH, D_MODEL)),
        'g_ffn_post': gain(ks[8], (DEPTH, D_MODEL)),
        'pool_w': nrm(ks[9], (N_POOL, POOL_GROUPS, POOL_GW, POOL_GW), POOL_GW ** -0.5),
        'pool_scale': 1.0 + 0.1 * jax.random.normal(ks[10], (N_POOL, D_MODEL), jnp.float32),
        'sgu_w_in': nrm(ks[11], (N_SGU, D_MODEL, 2 * SGU_WIDTH), D_MODEL ** -0.5),
        'sgu_b_in': nrm(ks[12], (N_SGU, 2 * SGU_WIDTH), 0.02),
        'sgu_g_v': gain(ks[13], (N_SGU, SGU_WIDTH)),
        'sgu_w_s': nrm(ks[14], (N_SGU, SGU_GROUPS, SGU_BLOCK, SGU_BLOCK), SGU_BLOCK ** -0.5),
        'sgu_b_s': 1.0 + 0.02 * jax.random.normal(ks[15], (N_SGU, SGU_GROUPS, SGU_BLOCK), jnp.float32),
        'sgu_w_out': nrm(ks[16], (N_SGU, SGU_WIDTH, D_MODEL), SGU_WIDTH ** -0.5),
        'sb_w_qkv': nrm(ks[17], (N_SB, D_MODEL, 3 * D_MODEL), D_MODEL ** -0.5),
        'sb_w_o': nrm(ks[18], (N_SB, D_MODEL, D_MODEL), D_MODEL ** -0.5),
        'ffn_w_up': nrm(ks[19], (DEPTH, D_MODEL, D_FF), D_MODEL ** -0.5),
        'ffn_w_down': nrm(ks[20], (DEPTH, D_FF, D_MODEL), D_FF ** -0.5),
    }


def reference(x_prompt, x_sample, cache_pool, cache_k, cache_v,
              g_mix_pre, g_mix_post, g_ffn_pre, g_ffn_post,
              pool_w, pool_scale,
              sgu_w_in, sgu_b_in, sgu_g_v, sgu_w_s, sgu_b_s, sgu_w_out,
              sb_w_qkv, sb_w_o, ffn_w_up, ffn_w_down):
    yp, ys = x_prompt, x_sample
    keep = min(x_prompt.shape[1], cache_k.shape[2])
    zero_hist = jnp.zeros((x_prompt.shape[0], POOL_HIST, D_MODEL), x_prompt.dtype)
    hist_off = jnp.zeros((POOL_HIST,), jnp.float32)
    hist_on = jnp.ones((POOL_HIST,), jnp.float32)
    pool_hist_p, pool_hist_s, sgu_v_s = [], [], []
    sb_k_p, sb_v_p, sb_k_s, sb_v_s = [], [], [], []
    for i in range(DEPTH):
        kind = i % N_MIXERS
        j = i // N_MIXERS
        hp = rms_norm(yp, g_mix_pre[i])
        hs = rms_norm(ys, g_mix_pre[i])
        if kind == 0:
            mp, hist_p = pool_mixer(hp, zero_hist, hist_off, pool_w[j], pool_scale[j])
            ms, hist_s = pool_mixer(hs, cache_pool[j], hist_on, pool_w[j], pool_scale[j])
            pool_hist_p.append(hist_p)
            pool_hist_s.append(hist_s)
        elif kind == 1:
            mp, _ = sgu_mixer(hp, sgu_w_in[j], sgu_b_in[j], sgu_g_v[j], sgu_w_s[j], sgu_b_s[j], sgu_w_out[j])
            ms, v_s = sgu_mixer(hs, sgu_w_in[j], sgu_b_in[j], sgu_g_v[j], sgu_w_s[j], sgu_b_s[j], sgu_w_out[j])
            sgu_v_s.append(v_s)
        else:
            mp, kp, vp = sb_prompt(hp, sb_w_qkv[j], sb_w_o[j])
            ms, ks_, vs_ = sb_sample(hs, cache_k[j], cache_v[j], sb_w_qkv[j], sb_w_o[j])
            sb_k_p.append(kp[:, -keep:])
            sb_v_p.append(vp[:, -keep:])
            sb_k_s.append(ks_)
            sb_v_s.append(vs_)
        yp = yp + rms_norm(mp, g_mix_post[i])
        ys = ys + rms_norm(ms, g_mix_post[i])
        yp = yp + rms_norm(sq_relu_mlp(rms_norm(yp, g_ffn_pre[i]), ffn_w_up[i], ffn_w_down[i]), g_ffn_post[i])
        ys = ys + rms_norm(sq_relu_mlp(rms_norm(ys, g_ffn_pre[i]), ffn_w_up[i], ffn_w_down[i]), g_ffn_post[i])
    return (yp, ys,
            jnp.stack(pool_hist_p), jnp.stack(pool_hist_s), jnp.stack(sgu_v_s),
            jnp.stack(sb_k_p), jnp.stack(sb_v_p), jnp.stack(sb_k_s), jnp.stack(sb_v_s))
```

```python
import functools

import jax
import jax.numpy as jnp
from jax import lax
from jax.experimental import pallas as pl
from jax.experimental.pallas import tpu as pltpu

F32 = jnp.float32
BF16 = jnp.bfloat16

NORM_EPS = 1e-6
CHUNK = 64
POOL_WINDOWS = (2, 4, 8, 16)
POOL_HIST = 15
HALO = 16
SGU_BLOCK = 128
SB_QBLOCK = 256
SB_CUM = 256

LANE = 128
BF16_SUBLANE = 16
VMEM_CAP = 60 * 1024 * 1024
MIB = 1024 * 1024


def _pick(n, cap, mult):
    best = None
    for d in range(mult, min(n, cap) + 1, mult):
        if n % d == 0:
            best = d
    assert best is not None, (n, cap, mult)
    return best


def _params(sem, vmem_bytes):
    return pltpu.CompilerParams(dimension_semantics=sem,
                                vmem_limit_bytes=int(min(VMEM_CAP, vmem_bytes)))


def _rms(x, g):
    ms = jnp.mean(x * x, axis=-1, keepdims=True)
    return x * lax.rsqrt(ms + NORM_EPS) * g


def _mm_kernel(*refs, nk, act, has_bias, n_out):
    x_ref, w_ref = refs[0], refs[1]
    b_ref = refs[2] if has_bias else None
    outs = refs[2 + has_bias:2 + has_bias + n_out]
    acc_ref = refs[2 + has_bias + n_out] if nk > 1 else None

    part = jnp.dot(x_ref[...], w_ref[...].astype(BF16), preferred_element_type=F32)

    def finish(acc):
        if has_bias:
            acc = acc + b_ref[...]
        if act == "relu2":
            r = jnp.maximum(acc, 0.0)
            acc = r * r
        elif act == "gelu":
            acc = 0.5 * acc * (1.0 + lax.erf(acc * (2.0 ** -0.5)))
        for o in outs:
            o[...] = acc.astype(o.dtype)

    if nk == 1:
        finish(part)
    else:
        k = pl.program_id(2)

        @pl.when(k == 0)
        def _():
            acc_ref[...] = part

        @pl.when(k > 0)
        def _():
            acc_ref[...] += part

        @pl.when(k == nk - 1)
        def _():
            finish(acc_ref[...])


def _matmul(x, w, layer, n_off, n_cols, out_dtypes, act="none", bias=None):
    M, K = x.shape
    tm = _pick(M, 1088, BF16_SUBLANE)
    tn = _pick(n_cols, 1024, LANE)
    tk = _pick(K, 4096, LANE)
    nk = K // tk
    assert n_off % tn == 0
    nb_off = n_off // tn
    wb = jnp.dtype(w.dtype).itemsize
    in_specs = [
        pl.BlockSpec((tm, tk), lambda n, m, k: (m, k)),
        pl.BlockSpec((None, tk, tn), lambda n, m, k: (layer, k, n + nb_off)),
    ]
    args = [x, w]
    if bias is not None:
        in_specs.append(pl.BlockSpec((None, 1, tn), lambda n, m, k: (layer, 0, n + nb_off)))
        args.append(bias)
    out_bytes = sum(jnp.dtype(d).itemsize for d in out_dtypes)
    vmem = (2 * tm * tk * 2 + 2 * tk * tn * wb + 2 * tm * tn * out_bytes
            + (2 + (nk > 1)) * tm * tn * 4 + (tk * tn * 2 if wb != 2 else 0) + 4 * MIB)
    return pl.pallas_call(
        functools.partial(_mm_kernel, nk=nk, act=act, has_bias=bias is not None,
                          n_out=len(out_dtypes)),
        out_shape=[jax.ShapeDtypeStruct((M, n_cols), d) for d in out_dtypes],
        grid=(n_cols // tn, M // tm, nk),
        in_specs=in_specs,
        out_specs=[pl.BlockSpec((tm, tn), lambda n, m, k: (m, n)) for _ in out_dtypes],
        scratch_shapes=[pltpu.VMEM((tm, tn), F32)] if nk > 1 else [],
        compiler_params=_params(("parallel", "parallel", "arbitrary"), vmem),
        name="matmul_" + act,
    )(*args)


def _resnorm_kernel(*refs, has_next):
    y_ref, m_ref, gpost_ref = refs[0], refs[1], refs[2]
    gnext_ref = refs[3] if has_next else None
    ynew_ref = refs[3 + has_next]
    ynew = y_ref[...] + _rms(m_ref[...], gpost_ref[...])
    ynew_ref[...] = ynew
    if has_next:
        refs[5][...] = _rms(ynew, gnext_ref[...]).astype(BF16)


def _resnorm(y, m, g_post, g_next=None, row_off=0, rows=None):
    D = y.shape[1]
    rows = y.shape[0] if rows is None else rows
    tm = _pick(rows, 256, BF16_SUBLANE)
    assert row_off % tm == 0
    ob = row_off // tm
    has_next = g_next is not None
    row = pl.BlockSpec((tm, D), lambda i: (i + ob, 0))
    out_row = pl.BlockSpec((tm, D), lambda i: (i, 0))
    vec = pl.BlockSpec((1, D), lambda i: (0, 0))
    out_shape = [jax.ShapeDtypeStruct((rows, D), F32)]
    if has_next:
        out_shape.append(jax.ShapeDtypeStruct((rows, D), BF16))
    vmem = 2 * tm * D * (4 + 4 + 4 + 2) + 4 * tm * D * 4 + 4 * MIB
    return pl.pallas_call(
        functools.partial(_resnorm_kernel, has_next=has_next),
        out_shape=out_shape,
        grid=(rows // tm,),
        in_specs=[row, row, vec] + ([vec] if has_next else []),
        out_specs=[out_row] * len(out_shape),
        compiler_params=_params(("parallel",), vmem),
        name="resnorm",
    )(*([y, m, g_post] + ([g_next] if has_next else [])))


def _pool_kernel(y_ref, hist_ref, gpre_ref, w_ref, scale_ref, gpost_ref, gnext_ref,
                 ynew_ref, hnext_ref, histout_ref, buf_ref, m_ref, *, tm, hist_valid):
    t = pl.program_id(1)
    D = y_ref.shape[-1]
    gw = D // len(POOL_WINDOWS)

    @pl.when(t == 0)
    def _():
        buf_ref[0:HALO, :] = hist_ref[...]

    @pl.when(t > 0)
    def _():
        buf_ref[0:HALO, :] = buf_ref[tm:tm + HALO, :]

    y = y_ref[...]
    h = _rms(y, gpre_ref[...])
    buf_ref[HALO:HALO + tm, :] = h
    histout_ref[...] = h[tm - HALO:, :]

    pos = t * tm + lax.broadcasted_iota(jnp.int32, (tm, 1), 0)
    ssq = jnp.zeros((tm, 1), F32)
    for g, w in enumerate(POOL_WINDOWS):
        cols = slice(g * gw, (g + 1) * gw)
        hg = buf_ref[HALO:HALO + tm, cols]
        s = hg
        for i in range(1, w):
            s = s + buf_ref[HALO - i:HALO - i + tm, cols]
        if hist_valid:
            mean = s / float(w)
        else:
            mean = s / jnp.minimum(pos + 1, w).astype(F32)
        d = mean - hg
        mg = jnp.dot(d.astype(BF16), w_ref[g], preferred_element_type=F32) * scale_ref[:, cols]
        m_ref[:, cols] = mg
        ssq = ssq + jnp.sum(mg * mg, axis=-1, keepdims=True)
    ynew = y + m_ref[...] * lax.rsqrt(ssq / D + NORM_EPS) * gpost_ref[...]
    ynew_ref[...] = ynew
    hnext_ref[...] = _rms(ynew, gnext_ref[...]).astype(BF16)


def _pool_segment(y, row_off, nseq, T, hist, hist_valid, out_rows, out_off, prev_outs,
                  g_pre, w_bf, scale, g_post, g_next):
    D = y.shape[1]
    G = len(POOL_WINDOWS)
    gw = D // G
    tm = _pick(T, 256, BF16_SUBLANE)
    nt = T // tm
    assert row_off % tm == 0 and out_off % tm == 0
    ib, ob = row_off // tm, out_off // tm
    vec = pl.BlockSpec((1, D), lambda s, t: (0, 0))
    in_specs = [
        pl.BlockSpec((tm, D), lambda s, t: (ib + s * nt + t, 0)),
        pl.BlockSpec((None, HALO, D), lambda s, t: (s, 0, 0)),
        vec,
        pl.BlockSpec((G, gw, gw), lambda s, t: (0, 0, 0)),
        vec, vec, vec,
    ]
    args = [y, hist, g_pre, w_bf, scale, g_post, g_next]
    aliases = {}
    if prev_outs is not None:
        in_specs += [pl.BlockSpec(memory_space=pl.ANY)] * 2
        args += list(prev_outs)
        aliases = {7: 0, 8: 1}
    out_row = pl.BlockSpec((tm, D), lambda s, t: (ob + s * nt + t, 0))
    vmem = (2 * tm * D * (4 + 4 + 2) + 2 * G * gw * gw * 2 + 2 * (tm + HALO) * D * 4
            + 3 * tm * D * 4 + 4 * MIB)
    kern = functools.partial(_pool_kernel, tm=tm, hist_valid=hist_valid)
    if prev_outs is not None:
        def kern(*refs, _k=kern):
            return _k(*refs[:7], *refs[9:])
    return pl.pallas_call(
        kern,
        out_shape=[jax.ShapeDtypeStruct((out_rows, D), F32),
                   jax.ShapeDtypeStruct((out_rows, D), BF16),
                   jax.ShapeDtypeStruct((nseq, HALO, D), F32)],
        grid=(nseq, nt),
        in_specs=in_specs,
        out_specs=[out_row, out_row, pl.BlockSpec((None, HALO, D), lambda s, t: (s, 0, 0))],
        scratch_shapes=[pltpu.VMEM((tm + HALO, D), F32), pltpu.VMEM((tm, D), F32)],
        input_output_aliases=aliases,
        compiler_params=_params(("parallel", "arbitrary"), vmem),
        name="pool_mixer",
    )(*args)


def _sgu_gate_kernel(*refs, blk, nblk, groups, has_prev, emit_v):
    u_ref, v_ref, gv_ref, ws_ref, bst_ref = refs[:5]
    outs = refs[5 + has_prev:]
    out_ref = outs[0]
    vout_ref = outs[1] if emit_v else None
    W = u_ref.shape[-1]
    gw = W // groups
    ri = lax.broadcasted_iota(jnp.int32, (SGU_BLOCK, SGU_BLOCK), 0)
    ci = lax.broadcasted_iota(jnp.int32, (SGU_BLOCK, SGU_BLOCK), 1)
    mask = (ci // CHUNK) <= (ri // CHUNK)
    wm = [jnp.where(mask, ws_ref[g], 0.0).astype(BF16)[:blk, :] for g in range(groups)]
    bias = [bst_ref[:blk, g:g + 1] for g in range(groups)]
    for b in range(nblk):
        rows = slice(b * blk, (b + 1) * blk)
        v = _rms(v_ref[rows, :], gv_ref[...])
        if emit_v:
            vout_ref[rows, :] = v
        vb = v.astype(BF16)
        if blk < SGU_BLOCK:
            vb = jnp.concatenate([vb, jnp.zeros((SGU_BLOCK - blk, W), BF16)], axis=0)
        for g in range(groups):
            cols = slice(g * gw, (g + 1) * gw)
            f = jnp.dot(wm[g], vb[:, cols], preferred_element_type=F32) + bias[g]
            out_ref[rows, cols] = (u_ref[rows, cols].astype(F32) * f).astype(BF16)


def _sgu_gate_segment(u, v_raw, row_off, rows, blk, g_v, w_s, b_s_t, prev_out, emit_v):
    M, W = u.shape
    groups = w_s.shape[0]
    tm = _pick(rows, 256, blk)
    nblk = tm // blk
    assert row_off % tm == 0
    ob = row_off // tm
    row = lambda i: (i + ob, 0)
    in_specs = [
        pl.BlockSpec((tm, W), row),
        pl.BlockSpec((tm, W), row),
        pl.BlockSpec((1, W), lambda i: (0, 0)),
        pl.BlockSpec((groups, SGU_BLOCK, SGU_BLOCK), lambda i: (0, 0, 0)),
        pl.BlockSpec((SGU_BLOCK, groups), lambda i: (0, 0)),
    ]
    args = [u, v_raw, g_v, w_s, b_s_t]
    aliases = {}
    if prev_out is not None:
        in_specs.append(pl.BlockSpec(memory_space=pl.ANY))
        args.append(prev_out)
        aliases = {5: 0}
    out_shape = [jax.ShapeDtypeStruct((M, W), BF16)]
    out_specs = [pl.BlockSpec((tm, W), row)]
    if emit_v:
        out_shape.append(jax.ShapeDtypeStruct((rows, W), F32))
        out_specs.append(pl.BlockSpec((tm, W), lambda i: (i, 0)))
    vmem = 2 * tm * W * (2 + 4 + 2 + 4) + 4 * tm * W * 4 + 4 * MIB
    return pl.pallas_call(
        functools.partial(_sgu_gate_kernel, blk=blk, nblk=nblk, groups=groups,
                          has_prev=prev_out is not None, emit_v=emit_v),
        out_shape=out_shape,
        grid=(rows // tm,),
        in_specs=in_specs,
        out_specs=out_specs,
        input_output_aliases=aliases,
        compiler_params=_params(("parallel",), vmem),
        name="sgu_gate",
    )(*args)


def _sb_logs(z):
    t = jnp.log1p(jnp.exp(-jnp.abs(z)))
    return jnp.minimum(z, 0.0) - t, -jnp.maximum(z, 0.0) - t


def _suffix_matrix(n):
    j = lax.broadcasted_iota(jnp.int32, (n, n), 0)
    s = lax.broadcasted_iota(jnp.int32, (n, n), 1)
    return jnp.where(j > s, 1.0, 0.0).astype(BF16)


def _suffix_sum(ln, U):
    n = ln.shape[0]
    hi = ln.astype(BF16)
    lo = (ln - hi.astype(F32)).astype(BF16)
    r = jnp.dot(jnp.concatenate([hi, lo], axis=0), U, preferred_element_type=F32)
    return r[:n] + r[n:]


def _sb_prompt_kernel(q_ref, k_ref, v_ref, o_ref, *, tq, scale):
    T = q_ref.shape[0]
    U = _suffix_matrix(tq)
    qi = lax.broadcasted_iota(jnp.int32, (tq, tq), 0)
    ki = lax.broadcasted_iota(jnp.int32, (tq, tq), 1)
    allowed = ki < qi

    def tile(q, start, c, acc, diagonal):
        k = k_ref[pl.ds(start, tq), :]
        v = v_ref[pl.ds(start, tq), :]
        z = lax.dot_general(q, k, (((1,), (1,)), ((), ())), preferred_element_type=F32) * scale
        ls, ln = _sb_logs(z)
        if diagonal:
            ln = jnp.where(allowed, ln, 0.0)
        a = jnp.exp(ls + _suffix_sum(ln, U) + c)
        if diagonal:
            a = jnp.where(allowed, a, 0.0)
        acc = acc + jnp.dot(a.astype(BF16), v, preferred_element_type=F32)
        c = c + jnp.sum(ln, axis=1, keepdims=True)
        return c, acc

    def q_body(i, carry):
        start = pl.multiple_of(i * tq, tq)
        q = q_ref[pl.ds(start, tq), :]
        c, acc = tile(q, start, jnp.zeros((tq, 1), F32), jnp.zeros((tq, q_ref.shape[1]), F32), True)

        def k_body(jj, ca):
            ks = pl.multiple_of((i - 1 - jj) * tq, tq)
            return tile(q, ks, ca[0], ca[1], False)

        c, acc = lax.fori_loop(0, i, k_body, (c, acc))
        o_ref[pl.ds(start, tq), :] = acc.astype(o_ref.dtype)
        return carry

    lax.fori_loop(0, T // tq, q_body, 0)


def _sb_prompt(q, k, v, B, T, H, dh, out_rows):
    tq = _pick(T, SB_QBLOCK, LANE)
    blk = pl.BlockSpec((T, dh), lambda b, h: (b, h))
    vmem = 2 * 4 * T * dh * 2 + 24 * tq * tq * 4 + 4 * MIB
    return pl.pallas_call(
        functools.partial(_sb_prompt_kernel, tq=tq, scale=dh ** -0.5),
        out_shape=jax.ShapeDtypeStruct((out_rows, H * dh), BF16),
        grid=(B, H),
        in_specs=[blk, blk, blk],
        out_specs=blk,
        compiler_params=_params(("parallel", "parallel"), vmem),
        name="sb_prompt",
    )(q, k, v)


def _sb_sample_kernel(q_ref, kn_ref, vn_ref, kc_ref, vc_ref, prev_ref, o_ref, *, scale):
    del prev_ref
    Ts, dh = q_ref.shape
    P = kc_ref.shape[0]
    cw = _pick(P, SB_CUM, LANE)
    q = q_ref[...]
    dn = (((1,), (1,)), ((), ()))

    qi = lax.broadcasted_iota(jnp.int32, (Ts, Ts), 0)
    ki = lax.broadcasted_iota(jnp.int32, (Ts, Ts), 1)
    allowed = ki < qi
    z = lax.dot_general(q, kn_ref[...], dn, preferred_element_type=F32) * scale
    ls, ln = _sb_logs(z)
    ln = jnp.where(allowed, ln, 0.0)
    a = jnp.where(allowed, jnp.exp(ls + _suffix_sum(ln, _suffix_matrix(Ts))), 0.0)
    acc = jnp.dot(a.astype(BF16), vn_ref[...], preferred_element_type=F32)
    c = jnp.sum(ln, axis=1, keepdims=True)

    U = _suffix_matrix(cw)
    z = lax.dot_general(q, kc_ref[...].astype(BF16), dn, preferred_element_type=F32) * scale
    ls, ln = _sb_logs(z)
    parts = [None] * (P // cw)
    for j in reversed(range(P // cw)):
        cols = slice(j * cw, (j + 1) * cw)
        lnj = ln[:, cols]
        parts[j] = jnp.exp(ls[:, cols] + _suffix_sum(lnj, U) + c).astype(BF16)
        c = c + jnp.sum(lnj, axis=1, keepdims=True)
    a = jnp.concatenate(parts, axis=1)
    acc = acc + jnp.dot(a, vc_ref[...].astype(BF16), preferred_element_type=F32)
    o_ref[...] = acc.astype(o_ref.dtype)


def _sb_sample(q, k, v, cache_k, cache_v, layer, row_off, Bs, Ts, H, dh, prev_out):
    P = cache_k.shape[2]
    cache_k = cache_k.reshape(cache_k.shape[:3] + (H * dh,))
    cache_v = cache_v.reshape(cache_v.shape[:3] + (H * dh,))
    assert row_off % Ts == 0
    ob = row_off // Ts
    new = pl.BlockSpec((Ts, dh), lambda b, h: (ob + b, h))
    old = pl.BlockSpec((None, None, P, dh), lambda b, h: (layer, b, 0, h))
    vmem = 2 * 2 * P * dh * 4 + 16 * Ts * P * 4 + 8 * MIB
    return pl.pallas_call(
        functools.partial(_sb_sample_kernel, scale=dh ** -0.5),
        out_shape=jax.ShapeDtypeStruct(prev_out.shape, BF16),
        grid=(Bs, H),
        in_specs=[new, new, new, old, old, pl.BlockSpec(memory_space=pl.ANY)],
        out_specs=new,
        input_output_aliases={5: 0},
        compiler_params=_params(("parallel", "parallel"), vmem),
        name="sb_sample",
    )(q, k, v, cache_k, cache_v, prev_out)


def kernel(x_prompt, x_sample, cache_pool, cache_k, cache_v, g_mix_pre, g_mix_post, g_ffn_pre, g_ffn_post, pool_w, pool_scale, sgu_w_in, sgu_b_in, sgu_g_v, sgu_w_s, sgu_b_s, sgu_w_out, sb_w_qkv, sb_w_o, ffn_w_up, ffn_w_down):
    B, T, D = x_prompt.shape
    Bs, Ts, _ = x_sample.shape
    Mp, Ms = B * T, Bs * Ts
    M = Mp + Ms
    depth = g_mix_pre.shape[0]
    H, dh = cache_k.shape[3], cache_k.shape[4]
    P = cache_k.shape[2]
    keep = min(T, P)
    W = sgu_g_v.shape[1]

    vec = lambda a, i: a[i].reshape(1, -1)
    to_bf16 = lambda a: a.astype(BF16)
    pool_w_bf, sgu_w_in_bf, sgu_w_out_bf = to_bf16(pool_w), to_bf16(sgu_w_in), to_bf16(sgu_w_out)
    sb_w_qkv_bf, sb_w_o_bf = to_bf16(sb_w_qkv), to_bf16(sb_w_o)
    ffn_w_up_bf, ffn_w_down_bf = to_bf16(ffn_w_up), to_bf16(ffn_w_down)

    zero_hist = jnp.zeros((B, HALO, D), F32)
    pad_hist = lambda c: jnp.pad(c, ((0, 0), (HALO - POOL_HIST, 0), (0, 0)))

    y = None
    h = None
    pool_hist_p, pool_hist_s, sgu_v_s = [], [], []
    sb_k_p, sb_v_p, sb_k_s, sb_v_s = [], [], [], []
    for i in range(depth):
        kind, j = i % 3, i // 3
        g_ffn = vec(g_ffn_pre, i)
        if kind == 0:
            if y is None:
                src_p, off_p, src_s, off_s = x_prompt.reshape(Mp, D), 0, x_sample.reshape(Ms, D), 0
            else:
                src_p, off_p, src_s, off_s = y, 0, y, Mp
            common = (vec(g_mix_pre, i), pool_w_bf[j], vec(pool_scale, j), vec(g_mix_post, i), g_ffn)
            y_p, h_p, hist_p = _pool_segment(src_p, off_p, B, T, zero_hist, False, M, 0, None, *common)
            y, h, hist_s = _pool_segment(src_s, off_s, Bs, Ts, pad_hist(cache_pool[j]), True, M, Mp,
                                         (y_p, h_p), *common)
            pool_hist_p.append(hist_p[:, HALO - POOL_HIST:])
            pool_hist_s.append(hist_s[:, HALO - POOL_HIST:])
        else:
            if kind == 1:
                b_in = sgu_b_in.reshape(sgu_b_in.shape[0], 1, -1)
                (u,) = _matmul(h, sgu_w_in_bf, j, 0, W, [BF16], act="gelu", bias=b_in)
                (v_raw,) = _matmul(h, sgu_w_in_bf, j, W, W, [F32], act="gelu", bias=b_in)
                gate_args = (vec(sgu_g_v, j), sgu_w_s[j], sgu_b_s[j].T)
                (gated,) = _sgu_gate_segment(u, v_raw, 0, Mp, min(T, SGU_BLOCK), *gate_args, None, False)
                gated, v_s = _sgu_gate_segment(u, v_raw, Mp, Ms, min(Ts, SGU_BLOCK), *gate_args, gated, True)
                sgu_v_s.append(v_s.reshape(Bs, Ts, W))
                (m,) = _matmul(gated, sgu_w_out_bf, j, 0, D, [F32])
            else:
                (q,) = _matmul(h, sb_w_qkv_bf, j, 0, D, [BF16])
                k32, k = _matmul(h, sb_w_qkv_bf, j, D, D, [F32, BF16])
                v32, v = _matmul(h, sb_w_qkv_bf, j, 2 * D, D, [F32, BF16])
                o = _sb_prompt(q, k, v, B, T, H, dh, M)
                o = _sb_sample(q, k, v, cache_k, cache_v, j, Mp, Bs, Ts, H, dh, o)
                (m,) = _matmul(o, sb_w_o_bf, j, 0, D, [F32])
                sb_k_p.append(k32[:Mp].reshape(B, T, H, dh)[:, -keep:])
                sb_v_p.append(v32[:Mp].reshape(B, T, H, dh)[:, -keep:])
                sb_k_s.append(k32[Mp:].reshape(Bs, Ts, H, dh))
                sb_v_s.append(v32[Mp:].reshape(Bs, Ts, H, dh))
            y, h = _resnorm(y, m, vec(g_mix_post, i), g_ffn)
        (a,) = _matmul(h, ffn_w_up_bf, i, 0, ffn_w_up.shape[2], [BF16], act="relu2")
        (m,) = _matmul(a, ffn_w_down_bf, i, 0, D, [F32])
        g_post = vec(g_ffn_post, i)
        if i + 1 == depth:
            (yp,) = _resnorm(y, m, g_post, None, 0, Mp)
            (ys,) = _resnorm(y, m, g_post, None, Mp, Ms)
        elif (i + 1) % 3 == 0:
            (y,) = _resnorm(y, m, g_post)
        else:
            y, h = _resnorm(y, m, g_post, vec(g_mix_pre, i + 1))
    return (yp.reshape(B, T, D), ys.reshape(Bs, Ts, D),
            jnp.stack(pool_hist_p), jnp.stack(pool_hist_s), jnp.stack(sgu_v_s),
            jnp.stack(sb_k_p), jnp.stack(sb_v_p), jnp.stack(sb_k_s), jnp.stack(sb_v_s))
```

```python
import functools

import jax
import jax.numpy as jnp
from jax import lax
from jax.experimental import pallas as pl
from jax.experimental.pallas import tpu as pltpu

F32 = jnp.float32
BF16 = jnp.bfloat16

NORM_EPS = 1e-6
CHUNK = 64
POOL_WINDOWS = (2, 4, 8, 16)
POOL_HIST = 15
HALO = 16
SGU_BLOCK = 128
MM_ROWS = 1088
MM_COLS = 1024
MM_DEPTH = 4096
SB_QBLOCK = 256
SB_CUM = 256
SB_PROMPT_HEADS = 4
SB_SAMPLE_KEYS = 1024
SB_SAMPLE_BATCH = 4

LANE = 128
F32_SUBLANE = 8
BF16_SUBLANE = 16
VMEM_CAP = 60 * 1024 * 1024
MIB = 1024 * 1024


def _pick(n, cap, mult):
    best = None
    for d in range(mult, min(n, cap) + 1, mult):
        if n % d == 0:
            best = d
    assert best is not None, (n, cap, mult)
    return best


def _params(sem, vmem_bytes):
    return pltpu.CompilerParams(dimension_semantics=sem,
                                vmem_limit_bytes=int(min(VMEM_CAP, vmem_bytes)))


def _rms(x, g):
    ms = jnp.mean(x * x, axis=-1, keepdims=True)
    return x * lax.rsqrt(ms + NORM_EPS) * g


def _activate(acc, act):
    if act == "relu2":
        r = jnp.maximum(acc, 0.0)
        return r * r
    if act == "gelu":
        return 0.5 * acc * (1.0 + lax.erf(acc * (2.0 ** -0.5)))
    return acc


def _mm_kernel(x_ref, w_ref, o_ref, acc_ref, *, nk):
    k = pl.program_id(2)
    part = jnp.dot(x_ref[...], w_ref[...], preferred_element_type=F32)

    @pl.when(k == 0)
    def _():
        acc_ref[...] = part

    @pl.when(k > 0)
    def _():
        acc_ref[...] += part

    @pl.when(k == nk - 1)
    def _():
        o_ref[...] = acc_ref[...].astype(o_ref.dtype)


def _matmul_kgrid(x, w, out_dtype):
    M, K = x.shape
    N = w.shape[1]
    tm = _pick(M, MM_ROWS, BF16_SUBLANE)
    tn = _pick(N, MM_COLS, LANE)
    tk = _pick(K, MM_DEPTH, LANE)
    nk = K // tk
    vmem = (2 * tm * tk * 2 + 2 * tk * tn * 2 + 2 * tm * tn * jnp.dtype(out_dtype).itemsize
            + 3 * tm * tn * 4 + 4 * MIB)
    return pl.pallas_call(
        functools.partial(_mm_kernel, nk=nk),
        out_shape=jax.ShapeDtypeStruct((M, N), out_dtype),
        grid=(N // tn, M // tm, nk),
        in_specs=[pl.BlockSpec((tm, tk), lambda n, m, k: (m, k)),
                  pl.BlockSpec((tk, tn), lambda n, m, k: (k, n))],
        out_specs=pl.BlockSpec((tm, tn), lambda n, m, k: (m, n)),
        scratch_shapes=[pltpu.VMEM((tm, tn), F32)],
        compiler_params=_params(("parallel", "parallel", "arbitrary"), vmem),
        name="matmul_kgrid",
    )(x, w)


def _mm_ws_kernel(*refs, act, has_bias, n_out, has_side, rows):
    x_ref, w_ref = refs[0], refs[1]
    pos = 2
    b_ref = refs[pos] if has_bias else None
    pos += has_bias
    side_in = refs[pos] if has_side else None
    pos += has_side
    outs = refs[pos:pos + n_out]
    pos += n_out
    side_out = refs[pos] if has_side else None
    pos += has_side
    wbf_ref = refs[pos]
    n, m = pl.program_id(0), pl.program_id(1)
    last = pl.num_programs(0) - 1

    def multiply():
        acc = jnp.dot(x_ref[...], wbf_ref[(n - 1) % 2], preferred_element_type=F32)
        if has_bias:
            acc = acc + b_ref[...]
        acc = _activate(acc, act)
        for o in outs:
            o[...] = acc.astype(o.dtype)
        if has_side:
            side_out[...] = side_in[...].astype(side_out.dtype)

    def cast_next():
        r0 = pl.multiple_of(m * rows, rows)
        wbf_ref[n % 2, pl.ds(r0, rows), :] = w_ref[...].astype(BF16)

    @pl.when(n == 0)
    def _():
        cast_next()

    @pl.when(jnp.logical_and(n > 0, n < last))
    def _():
        multiply()
        cast_next()

    @pl.when(n == last)
    def _():
        multiply()


def _matmul(x, w, layer, n_off, n_cols, out_dtypes, act="none", bias=None, side=None):
    M, K = x.shape
    tm = _pick(M, MM_ROWS, BF16_SUBLANE)
    tn = _pick(n_cols, MM_COLS, LANE)
    mt, nt = M // tm, n_cols // tn
    rows = K // mt
    assert K <= MM_DEPTH and K % mt == 0 and rows % BF16_SUBLANE == 0 and n_off % tn == 0
    nb_off = n_off // tn
    row_of = lambda n, m: jnp.where(n > 0, m, 0)
    col_of = lambda n: jnp.maximum(n - 1, 0)
    in_specs = [
        pl.BlockSpec((tm, K), lambda n, m: (row_of(n, m), 0)),
        pl.BlockSpec((None, rows, tn),
                     lambda n, m: (layer, jnp.where(n < nt, m, mt - 1), jnp.minimum(n, nt - 1) + nb_off)),
    ]
    args = [x, w]
    if bias is not None:
        in_specs.append(pl.BlockSpec((None, 1, tn), lambda n, m: (layer, 0, col_of(n) + nb_off)))
        args.append(bias)
    out_shape = [jax.ShapeDtypeStruct((M, n_cols), d) for d in out_dtypes]
    out_specs = [pl.BlockSpec((tm, tn), lambda n, m: (row_of(n, m), col_of(n))) for _ in out_dtypes]
    out_bytes = sum(jnp.dtype(d).itemsize for d in out_dtypes)
    vmem = (2 * tm * K * 2 + 2 * K * tn * 2 + 2 * rows * tn * 4 + 2 * tm * tn * out_bytes
            + 2 * tm * tn * 4 + 4 * MIB)
    if side is not None:
        s_arr, s_idx = side
        R, C = s_arr.shape[1:]
        rs = R // (nt * mt)
        assert R % (nt * mt) == 0 and rs % BF16_SUBLANE == 0
        slab = lambda n, m: jnp.maximum((n - 1) * mt + m, 0)
        in_specs.append(pl.BlockSpec((None, rs, C), lambda n, m: (s_idx, slab(n, m), 0)))
        args.append(s_arr)
        out_shape.append(jax.ShapeDtypeStruct((R, C), BF16))
        out_specs.append(pl.BlockSpec((rs, C), lambda n, m: (slab(n, m), 0)))
        vmem += 2 * rs * C * (4 + 2)
    return pl.pallas_call(
        functools.partial(_mm_ws_kernel, act=act, has_bias=bias is not None, n_out=len(out_dtypes),
                          has_side=side is not None, rows=rows),
        out_shape=out_shape,
        grid=(nt + 1, mt),
        in_specs=in_specs,
        out_specs=out_specs,
        scratch_shapes=[pltpu.VMEM((2, K, tn), BF16)],
        compiler_params=_params(("arbitrary", "arbitrary"), vmem),
        name="matmul_" + act,
    )(*args)


def _resnorm_kernel(*refs, has_next):
    y_ref, m_ref, gpost_ref = refs[0], refs[1], refs[2]
    gnext_ref = refs[3] if has_next else None
    ynew_ref = refs[3 + has_next]
    ynew = y_ref[...] + _rms(m_ref[...], gpost_ref[...])
    ynew_ref[...] = ynew
    if has_next:
        refs[5][...] = _rms(ynew, gnext_ref[...]).astype(BF16)


def _resnorm(y, m, g_post, g_next=None, row_off=0, rows=None):
    D = y.shape[1]
    rows = y.shape[0] if rows is None else rows
    tm = _pick(rows, 256, BF16_SUBLANE)
    assert row_off % tm == 0
    ob = row_off // tm
    has_next = g_next is not None
    row = pl.BlockSpec((tm, D), lambda i: (i + ob, 0))
    out_row = pl.BlockSpec((tm, D), lambda i: (i, 0))
    vec = pl.BlockSpec((1, D), lambda i: (0, 0))
    out_shape = [jax.ShapeDtypeStruct((rows, D), F32)]
    if has_next:
        out_shape.append(jax.ShapeDtypeStruct((rows, D), BF16))
    vmem = 2 * tm * D * (4 + 4 + 4 + 2) + 4 * tm * D * 4 + 4 * MIB
    return pl.pallas_call(
        functools.partial(_resnorm_kernel, has_next=has_next),
        out_shape=out_shape,
        grid=(rows // tm,),
        in_specs=[row, row, vec] + ([vec] if has_next else []),
        out_specs=[out_row] * len(out_shape),
        compiler_params=_params(("parallel",), vmem),
        name="resnorm",
    )(*([y, m, g_post] + ([g_next] if has_next else [])))


def _pool_kernel(y_ref, hist_ref, gpre_ref, w_ref, scale_ref, gpost_ref, gnext_ref,
                 ynew_ref, hnext_ref, histout_ref, buf_ref, m_ref, *, tm, hist_valid):
    t = pl.program_id(1)
    D = y_ref.shape[-1]
    gw = D // len(POOL_WINDOWS)

    @pl.when(t == 0)
    def _():
        buf_ref[0:HALO, :] = hist_ref[...]

    @pl.when(t > 0)
    def _():
        buf_ref[0:HALO, :] = buf_ref[tm:tm + HALO, :]

    y = y_ref[...]
    h = _rms(y, gpre_ref[...])
    buf_ref[HALO:HALO + tm, :] = h
    histout_ref[...] = h[tm - HALO:, :]

    pos = t * tm + lax.broadcasted_iota(jnp.int32, (tm, 1), 0)
    ssq = jnp.zeros((tm, 1), F32)
    for g, w in enumerate(POOL_WINDOWS):
        cols = slice(g * gw, (g + 1) * gw)
        hg = buf_ref[HALO:HALO + tm, cols]
        s = hg
        for i in range(1, w):
            s = s + buf_ref[HALO - i:HALO - i + tm, cols]
        if hist_valid:
            mean = s / float(w)
        else:
            mean = s / jnp.minimum(pos + 1, w).astype(F32)
        d = mean - hg
        mg = jnp.dot(d.astype(BF16), w_ref[g], preferred_element_type=F32) * scale_ref[:, cols]
        m_ref[:, cols] = mg
        ssq = ssq + jnp.sum(mg * mg, axis=-1, keepdims=True)
    ynew = y + m_ref[...] * lax.rsqrt(ssq / D + NORM_EPS) * gpost_ref[...]
    ynew_ref[...] = ynew
    hnext_ref[...] = _rms(ynew, gnext_ref[...]).astype(BF16)


def _pool_segment(y, row_off, nseq, T, hist, hist_valid, out_rows, out_off, prev_outs,
                  g_pre, w_bf, scale, g_post, g_next):
    D = y.shape[1]
    G = len(POOL_WINDOWS)
    gw = D // G
    tm = _pick(T, 256, BF16_SUBLANE)
    nt = T // tm
    assert row_off % tm == 0 and out_off % tm == 0
    ib, ob = row_off // tm, out_off // tm
    vec = pl.BlockSpec((1, D), lambda s, t: (0, 0))
    in_specs = [
        pl.BlockSpec((tm, D), lambda s, t: (ib + s * nt + t, 0)),
        pl.BlockSpec((None, HALO, D), lambda s, t: (s, 0, 0)),
        vec,
        pl.BlockSpec((G, gw, gw), lambda s, t: (0, 0, 0)),
        vec, vec, vec,
    ]
    args = [y, hist, g_pre, w_bf, scale, g_post, g_next]
    aliases = {}
    if prev_outs is not None:
        in_specs += [pl.BlockSpec(memory_space=pl.ANY)] * 2
        args += list(prev_outs)
        aliases = {7: 0, 8: 1}
    out_row = pl.BlockSpec((tm, D), lambda s, t: (ob + s * nt + t, 0))
    vmem = (2 * tm * D * (4 + 4 + 2) + 2 * G * gw * gw * 2 + 2 * (tm + HALO) * D * 4
            + 3 * tm * D * 4 + 4 * MIB)
    kern = functools.partial(_pool_kernel, tm=tm, hist_valid=hist_valid)
    if prev_outs is not None:
        def kern(*refs, _k=kern):
            return _k(*refs[:7], *refs[9:])
    return pl.pallas_call(
        kern,
        out_shape=[jax.ShapeDtypeStruct((out_rows, D), F32),
                   jax.ShapeDtypeStruct((out_rows, D), BF16),
                   jax.ShapeDtypeStruct((nseq, HALO, D), F32)],
        grid=(nseq, nt),
        in_specs=in_specs,
        out_specs=[out_row, out_row, pl.BlockSpec((None, HALO, D), lambda s, t: (s, 0, 0))],
        scratch_shapes=[pltpu.VMEM((tm + HALO, D), F32), pltpu.VMEM((tm, D), F32)],
        input_output_aliases=aliases,
        compiler_params=_params(("parallel", "arbitrary"), vmem),
        name="pool_mixer",
    )(*args)


def _sgu_gate_kernel(*refs, blk, nblk, groups, has_prev, emit_v):
    u_ref, v_ref, gv_ref, ws_ref, bst_ref = refs[:5]
    outs = refs[5 + has_prev:]
    out_ref = outs[0]
    vout_ref = outs[1] if emit_v else None
    W = u_ref.shape[-1]
    gw = W // groups
    ri = lax.broadcasted_iota(jnp.int32, (SGU_BLOCK, SGU_BLOCK), 0)
    ci = lax.broadcasted_iota(jnp.int32, (SGU_BLOCK, SGU_BLOCK), 1)
    mask = (ci // CHUNK) <= (ri // CHUNK)
    wm = [jnp.where(mask, ws_ref[g], 0.0).astype(BF16)[:blk, :] for g in range(groups)]
    bias = [bst_ref[:blk, g:g + 1] for g in range(groups)]
    for b in range(nblk):
        rows = slice(b * blk, (b + 1) * blk)
        v = _rms(v_ref[rows, :], gv_ref[...])
        if emit_v:
            vout_ref[rows, :] = v
        vb = v.astype(BF16)
        if blk < SGU_BLOCK:
            vb = jnp.concatenate([vb, jnp.zeros((SGU_BLOCK - blk, W), BF16)], axis=0)
        for g in range(groups):
            cols = slice(g * gw, (g + 1) * gw)
            f = jnp.dot(wm[g], vb[:, cols], preferred_element_type=F32) + bias[g]
            out_ref[rows, cols] = (u_ref[rows, cols].astype(F32) * f).astype(BF16)


def _sgu_gate_segment(u, v_raw, row_off, rows, blk, g_v, w_s, b_s_t, prev_out, emit_v):
    M, W = u.shape
    groups = w_s.shape[0]
    tm = _pick(rows, 256, blk)
    nblk = tm // blk
    assert row_off % tm == 0
    ob = row_off // tm
    row = lambda i: (i + ob, 0)
    in_specs = [
        pl.BlockSpec((tm, W), row),
        pl.BlockSpec((tm, W), row),
        pl.BlockSpec((1, W), lambda i: (0, 0)),
        pl.BlockSpec((groups, SGU_BLOCK, SGU_BLOCK), lambda i: (0, 0, 0)),
        pl.BlockSpec((SGU_BLOCK, groups), lambda i: (0, 0)),
    ]
    args = [u, v_raw, g_v, w_s, b_s_t]
    aliases = {}
    if prev_out is not None:
        in_specs.append(pl.BlockSpec(memory_space=pl.ANY))
        args.append(prev_out)
        aliases = {5: 0}
    out_shape = [jax.ShapeDtypeStruct((M, W), BF16)]
    out_specs = [pl.BlockSpec((tm, W), row)]
    if emit_v:
        out_shape.append(jax.ShapeDtypeStruct((rows, W), F32))
        out_specs.append(pl.BlockSpec((tm, W), lambda i: (i, 0)))
    vmem = 2 * tm * W * (2 + 4 + 2 + 4) + 4 * tm * W * 4 + 4 * MIB
    return pl.pallas_call(
        functools.partial(_sgu_gate_kernel, blk=blk, nblk=nblk, groups=groups,
                          has_prev=prev_out is not None, emit_v=emit_v),
        out_shape=out_shape,
        grid=(rows // tm,),
        in_specs=in_specs,
        out_specs=out_specs,
        input_output_aliases=aliases,
        compiler_params=_params(("parallel",), vmem),
        name="sgu_gate",
    )(*args)


def _sb_logs(z):
    ls = jnp.minimum(z, 0.0) - jnp.log(1.0 + jnp.exp(-jnp.abs(z)))
    return ls, ls - z


def _suffix_matrix(n):
    j = lax.broadcasted_iota(jnp.int32, (n, n), 0)
    s = lax.broadcasted_iota(jnp.int32, (n, n), 1)
    return jnp.where(j > s, 1.0, 0.0).astype(BF16)


def _suffix_sum(ln, U):
    n = ln.shape[0]
    hi = ln.astype(BF16)
    lo = (ln - hi.astype(F32)).astype(BF16)
    r = jnp.dot(jnp.concatenate([hi, lo], axis=0), U, preferred_element_type=F32)
    return r[:n] + r[n:]


def _sb_tiles(qs, ks, vs, cs, U, scale, allowed=None):
    dn = (((1,), (1,)), ((), ()))
    zs = [lax.dot_general(q, k, dn, preferred_element_type=F32) * scale for q, k in zip(qs, ks)]
    cw = U.shape[0]
    chunks = [slice(j * cw, (j + 1) * cw) for j in range(zs[0].shape[1] // cw)]
    logs = []
    for z in zs:
        ls, ln = _sb_logs(z)
        if allowed is not None:
            ln = jnp.where(allowed, ln, 0.0)
        logs.append((ls, ln, [_suffix_sum(ln[:, cols], U) for cols in chunks]))
    outs, new_cs = [], []
    for (ls, ln, suffix), v, c in zip(logs, vs, cs):
        parts = [None] * len(chunks)
        for j in reversed(range(len(chunks))):
            parts[j] = jnp.exp(ls[:, chunks[j]] + suffix[j] + c)
            c = c + jnp.sum(ln[:, chunks[j]], axis=1, keepdims=True)
        a = parts[0] if len(parts) == 1 else jnp.concatenate(parts, axis=1)
        if allowed is not None:
            a = jnp.where(allowed, a, 0.0)
        outs.append(jnp.dot(a.astype(BF16), v, preferred_element_type=F32))
        new_cs.append(c)
    return outs, new_cs


def _sb_prompt_kernel(q_ref, k_ref, v_ref, o_ref, u_ref, c_ref, acc_ref, *, tq, heads, dh, scale):
    T = q_ref.shape[0]
    u_ref[...] = _suffix_matrix(tq)
    qi = lax.broadcasted_iota(jnp.int32, (tq, tq), 0)
    ki = lax.broadcasted_iota(jnp.int32, (tq, tq), 1)

    def tiles(qs, ks, diagonal):
        cols = [slice(g * dh, (g + 1) * dh) for g in range(heads)]
        outs, cs = _sb_tiles([q_ref[pl.ds(qs, tq), c] for c in cols],
                             [k_ref[pl.ds(ks, tq), c] for c in cols],
                             [v_ref[pl.ds(ks, tq), c] for c in cols],
                             [jnp.zeros((tq, 1), F32) if diagonal else c_ref[g] for g in range(heads)],
                             u_ref[...], scale, (ki < qi) if diagonal else None)
        for g in range(heads):
            c_ref[g] = cs[g]
            acc_ref[g] = outs[g] if diagonal else acc_ref[g] + outs[g]

    def q_body(i, carry):
        qs = pl.multiple_of(i * tq, tq)
        tiles(qs, qs, True)

        def k_body(jj, carry):
            tiles(qs, pl.multiple_of((i - 1 - jj) * tq, tq), False)
            return carry

        lax.fori_loop(0, i, k_body, 0)
        for g in range(heads):
            o_ref[pl.ds(qs, tq), g * dh:(g + 1) * dh] = acc_ref[g].astype(o_ref.dtype)
        return carry

    lax.fori_loop(0, T // tq, q_body, 0)


def _sb_prompt(q, k, v, B, T, H, dh, out_rows):
    tq = _pick(T, SB_QBLOCK, LANE)
    heads = _pick(H, SB_PROMPT_HEADS, 1)
    blk = pl.BlockSpec((T, heads * dh), lambda b, h: (b, h))
    vmem = 2 * 4 * T * heads * dh * 2 + heads * 24 * tq * tq * 4 + 4 * MIB
    return pl.pallas_call(
        functools.partial(_sb_prompt_kernel, tq=tq, heads=heads, dh=dh, scale=dh ** -0.5),
        out_shape=jax.ShapeDtypeStruct((out_rows, H * dh), BF16),
        grid=(B, H // heads),
        in_specs=[blk, blk, blk],
        out_specs=blk,
        scratch_shapes=[pltpu.VMEM((tq, tq), BF16), pltpu.VMEM((heads, tq, 1), F32),
                        pltpu.VMEM((heads, tq, dh), F32)],
        compiler_params=_params(("parallel", "parallel"), vmem),
        name="sb_prompt",
    )(q, k, v)


def _sb_sample_kernel(q_ref, kn_ref, vn_ref, kc_ref, vc_ref, prev_ref, o_ref, u_ref, c_ref, acc_ref,
                      *, heads, scale):
    del prev_ref
    pc = pl.program_id(2)
    Ts = q_ref.shape[0]
    dh = kc_ref.shape[2]
    k_all = pltpu.einshape("phd->hpd", kc_ref[...]).astype(BF16)
    v_all = pltpu.einshape("phd->hpd", vc_ref[...]).astype(BF16)
    cols = [slice(g * dh, (g + 1) * dh) for g in range(heads)]

    @pl.when(pc == 0)
    def _():
        u_ref[...] = _suffix_matrix(u_ref.shape[0])
        qi = lax.broadcasted_iota(jnp.int32, (Ts, Ts), 0)
        ki = lax.broadcasted_iota(jnp.int32, (Ts, Ts), 1)
        outs, cs = _sb_tiles([q_ref[:, c] for c in cols], [kn_ref[:, c] for c in cols],
                             [vn_ref[:, c] for c in cols], [jnp.zeros((Ts, 1), F32)] * heads,
                             _suffix_matrix(Ts), scale, ki < qi)
        for g in range(heads):
            c_ref[g] = cs[g]
            acc_ref[g] = outs[g]

    for g0 in range(0, heads, SB_SAMPLE_BATCH):
        gs = range(g0, g0 + SB_SAMPLE_BATCH)
        outs, cs = _sb_tiles([q_ref[:, cols[g]] for g in gs], [k_all[g] for g in gs],
                             [v_all[g] for g in gs], [c_ref[g] for g in gs], u_ref[...], scale)
        for g, o, c in zip(gs, outs, cs):
            c_ref[g] = c
            acc_ref[g] = acc_ref[g] + o

    @pl.when(pc == pl.num_programs(2) - 1)
    def _():
        for g in range(heads):
            o_ref[:, g * dh:(g + 1) * dh] = acc_ref[g].astype(o_ref.dtype)


def _sb_sample(q, k, v, cache_k, cache_v, layer, row_off, Bs, Ts, H, dh, prev_out):
    P = cache_k.shape[2]
    heads = F32_SUBLANE
    assert H % heads == 0 and row_off % Ts == 0
    cw = _pick(P, SB_CUM, LANE)
    tp = _pick(P, SB_SAMPLE_KEYS, cw)
    npc = P // tp
    ob = row_off // Ts
    new = pl.BlockSpec((Ts, heads * dh), lambda b, g, p: (ob + b, g))
    old = pl.BlockSpec((None, None, tp, heads, dh), lambda b, g, p: (layer, b, npc - 1 - p, g, 0))
    vmem = 2 * 2 * tp * heads * dh * 4 + heads * 16 * Ts * tp * 4 + 8 * MIB
    return pl.pallas_call(
        functools.partial(_sb_sample_kernel, heads=heads, scale=dh ** -0.5),
        out_shape=jax.ShapeDtypeStruct(prev_out.shape, BF16),
        grid=(Bs, H // heads, npc),
        in_specs=[new, new, new, old, old, pl.BlockSpec(memory_space=pl.ANY)],
        out_specs=new,
        scratch_shapes=[pltpu.VMEM((cw, cw), BF16), pltpu.VMEM((heads, Ts, 1), F32),
                        pltpu.VMEM((heads, Ts, dh), F32)],
        input_output_aliases={5: 0},
        compiler_params=_params(("parallel", "parallel", "arbitrary"), vmem),
        name="sb_sample",
    )(q, k, v, cache_k, cache_v, prev_out)


def kernel(x_prompt, x_sample, cache_pool, cache_k, cache_v, g_mix_pre, g_mix_post, g_ffn_pre, g_ffn_post, pool_w, pool_scale, sgu_w_in, sgu_b_in, sgu_g_v, sgu_w_s, sgu_b_s, sgu_w_out, sb_w_qkv, sb_w_o, ffn_w_up, ffn_w_down):
    B, T, D = x_prompt.shape
    Bs, Ts, _ = x_sample.shape
    Mp, Ms = B * T, Bs * Ts
    M = Mp + Ms
    depth = g_mix_pre.shape[0]
    H, dh = cache_k.shape[3], cache_k.shape[4]
    P = cache_k.shape[2]
    keep = min(T, P)
    W = sgu_g_v.shape[1]

    vec = lambda a, i: a[i].reshape(1, -1)
    pool_w_bf = pool_w.astype(BF16)

    zero_hist = jnp.zeros((B, HALO, D), F32)
    pad_hist = lambda c: jnp.pad(c, ((0, 0), (HALO - POOL_HIST, 0), (0, 0)))

    y = None
    h = None
    pool_hist_p, pool_hist_s, sgu_v_s = [], [], []
    sb_k_p, sb_v_p, sb_k_s, sb_v_s = [], [], [], []
    for i in range(depth):
        kind, j = i % 3, i // 3
        g_ffn = vec(g_ffn_pre, i)
        if kind == 0:
            if y is None:
                src_p, off_p, src_s, off_s = x_prompt.reshape(Mp, D), 0, x_sample.reshape(Ms, D), 0
            else:
                src_p, off_p, src_s, off_s = y, 0, y, Mp
            common = (vec(g_mix_pre, i), pool_w_bf[j], vec(pool_scale, j), vec(g_mix_post, i), g_ffn)
            y_p, h_p, hist_p = _pool_segment(src_p, off_p, B, T, zero_hist, False, M, 0, None, *common)
            y, h, hist_s = _pool_segment(src_s, off_s, Bs, Ts, pad_hist(cache_pool[j]), True, M, Mp,
                                         (y_p, h_p), *common)
            pool_hist_p.append(hist_p[:, HALO - POOL_HIST:])
            pool_hist_s.append(hist_s[:, HALO - POOL_HIST:])
        else:
            if kind == 1:
                b_in = sgu_b_in.reshape(sgu_b_in.shape[0], 1, -1)
                (u,) = _matmul(h, sgu_w_in, j, 0, W, [BF16], act="gelu", bias=b_in)
                (v_raw,) = _matmul(h, sgu_w_in, j, W, W, [F32], act="gelu", bias=b_in)
                gate_args = (vec(sgu_g_v, j), sgu_w_s[j], sgu_b_s[j].T)
                (gated,) = _sgu_gate_segment(u, v_raw, 0, Mp, min(T, SGU_BLOCK), *gate_args, None, False)
                gated, v_s = _sgu_gate_segment(u, v_raw, Mp, Ms, min(Ts, SGU_BLOCK), *gate_args, gated, True)
                sgu_v_s.append(v_s.reshape(Bs, Ts, W))
                (m,) = _matmul(gated, sgu_w_out, j, 0, D, [F32])
            else:
                (q,) = _matmul(h, sb_w_qkv, j, 0, D, [BF16])
                k32, k = _matmul(h, sb_w_qkv, j, D, D, [F32, BF16])
                v32, v = _matmul(h, sb_w_qkv, j, 2 * D, D, [F32, BF16])
                o = _sb_prompt(q, k, v, B, T, H, dh, M)
                o = _sb_sample(q, k, v, cache_k, cache_v, j, Mp, Bs, Ts, H, dh, o)
                (m,) = _matmul(o, sb_w_o, j, 0, D, [F32])
                sb_k_p.append(k32[:Mp].reshape(B, T, H, dh)[:, -keep:])
                sb_v_p.append(v32[:Mp].reshape(B, T, H, dh)[:, -keep:])
                sb_k_s.append(k32[Mp:].reshape(Bs, Ts, H, dh))
                sb_v_s.append(v32[Mp:].reshape(Bs, Ts, H, dh))
            y, h = _resnorm(y, m, vec(g_mix_post, i), g_ffn)
        a, w_down_bf = _matmul(h, ffn_w_up, i, 0, ffn_w_up.shape[2], [BF16], act="relu2",
                               side=(ffn_w_down, i))
        m = _matmul_kgrid(a, w_down_bf, F32)
        g_post = vec(g_ffn_post, i)
        if i + 1 == depth:
            (yp,) = _resnorm(y, m, g_post, None, 0, Mp)
            (ys,) = _resnorm(y, m, g_post, None, Mp, Ms)
        elif (i + 1) % 3 == 0:
            (y,) = _resnorm(y, m, g_post)
        else:
            y, h = _resnorm(y, m, g_post, vec(g_mix_pre, i + 1))
    return (yp.reshape(B, T, D), ys.reshape(Bs, Ts, D),
            jnp.stack(pool_hist_p), jnp.stack(pool_hist_s), jnp.stack(sgu_v_s),
            jnp.stack(sb_k_p), jnp.stack(sb_v_p), jnp.stack(sb_k_s), jnp.stack(sb_v_s))
```

```python
import functools
import math

import jax
import jax.numpy as jnp
from jax import lax
from jax.experimental import pallas as pl
from jax.experimental.pallas import tpu as pltpu

F32 = jnp.float32
BF16 = jnp.bfloat16

NORM_EPS = 1e-6
CHUNK = 64
POOL_WINDOWS = (2, 4, 8, 16)
POOL_HIST = 15
HALO = 16
PAD = 8
SGU_BLOCK = 128
MM_ROWS = 1088
MM_COLS = 1024
MM_DEPTH = 4096
SB_QBLOCK = 256
SB_CUM = 256
SB_PROMPT_HEADS = 4
SB_WIDE = 4
SB_SAMPLE_KEYS = 1024
SB_SAMPLE_BATCH = 4

LANE = 128
F32_SUBLANE = 8
BF16_SUBLANE = 16
VMEM_CAP = 60 * 1024 * 1024
MIB = 1024 * 1024


def _pick(n, cap, mult):
    best = None
    for d in range(mult, min(n, cap) + 1, mult):
        if n % d == 0:
            best = d
    assert best is not None, (n, cap, mult)
    return best


def _params(sem, vmem_bytes):
    return pltpu.CompilerParams(dimension_semantics=sem,
                                vmem_limit_bytes=int(min(VMEM_CAP, vmem_bytes)))


def _rms(x, g):
    ms = jnp.mean(x * x, axis=-1, keepdims=True)
    return x * lax.rsqrt(ms + NORM_EPS) * g


def _activate(acc, act):
    if act == "relu2":
        r = jnp.maximum(acc, 0.0)
        return r * r
    if act == "gelu":
        return 0.5 * acc * (1.0 + lax.erf(acc * (2.0 ** -0.5)))
    return acc


def _mm_kernel(x_ref, w_ref, o_ref, acc_ref, *, nk):
    k = pl.program_id(2)
    part = jnp.dot(x_ref[...], w_ref[...], preferred_element_type=F32)

    @pl.when(k == 0)
    def _():
        acc_ref[...] = part

    @pl.when(k > 0)
    def _():
        acc_ref[...] += part

    @pl.when(k == nk - 1)
    def _():
        o_ref[...] = acc_ref[...].astype(o_ref.dtype)


def _matmul_kgrid(x, w, out_dtype):
    M, K = x.shape
    N = w.shape[1]
    tm = _pick(M, MM_ROWS, BF16_SUBLANE)
    tn = _pick(N, MM_COLS, LANE)
    tk = _pick(K, MM_DEPTH, LANE)
    nk = K // tk
    vmem = (2 * tm * tk * 2 + 2 * tk * tn * 2 + 2 * tm * tn * jnp.dtype(out_dtype).itemsize
            + 3 * tm * tn * 4 + 4 * MIB)
    return pl.pallas_call(
        functools.partial(_mm_kernel, nk=nk),
        out_shape=jax.ShapeDtypeStruct((M, N), out_dtype),
        grid=(N // tn, M // tm, nk),
        in_specs=[pl.BlockSpec((tm, tk), lambda n, m, k: (m, k)),
                  pl.BlockSpec((tk, tn), lambda n, m, k: (k, n))],
        out_specs=pl.BlockSpec((tm, tn), lambda n, m, k: (m, n)),
        scratch_shapes=[pltpu.VMEM((tm, tn), F32)],
        compiler_params=_params(("parallel", "parallel", "arbitrary"), vmem),
        name="matmul_kgrid",
    )(x, w)


def _mm_ws_kernel(*refs, act, has_bias, n_out, has_side, rows):
    x_ref, w_ref = refs[0], refs[1]
    pos = 2
    b_ref = refs[pos] if has_bias else None
    pos += has_bias
    side_in = refs[pos] if has_side else None
    pos += has_side
    outs = refs[pos:pos + n_out]
    pos += n_out
    side_out = refs[pos] if has_side else None
    pos += has_side
    wbf_ref = refs[pos]
    n, m = pl.program_id(0), pl.program_id(1)
    last = pl.num_programs(0) - 1

    def multiply():
        acc = jnp.dot(x_ref[...], wbf_ref[(n - 1) % 2], preferred_element_type=F32)
        if has_bias:
            acc = acc + b_ref[...]
        acc = _activate(acc, act)
        for o in outs:
            o[...] = acc.astype(o.dtype)
        if has_side:
            side_out[...] = side_in[...].astype(side_out.dtype)

    def cast_next():
        r0 = pl.multiple_of(m * rows, rows)
        wbf_ref[n % 2, pl.ds(r0, rows), :] = w_ref[...].astype(BF16)

    @pl.when(n == 0)
    def _():
        cast_next()

    @pl.when(jnp.logical_and(n > 0, n < last))
    def _():
        multiply()
        cast_next()

    @pl.when(n == last)
    def _():
        multiply()


def _matmul(x, w, layer, n_off, n_cols, out_dtypes, act="none", bias=None, side=None):
    M, K = x.shape
    tm = _pick(M, MM_ROWS, BF16_SUBLANE)
    tn = _pick(n_cols, MM_COLS, LANE)
    mt, nt = M // tm, n_cols // tn
    rows = K // mt
    assert K <= MM_DEPTH and K % mt == 0 and rows % BF16_SUBLANE == 0 and n_off % tn == 0
    nb_off = n_off // tn
    row_of = lambda n, m: jnp.where(n > 0, m, 0)
    col_of = lambda n: jnp.maximum(n - 1, 0)
    in_specs = [
        pl.BlockSpec((tm, K), lambda n, m: (row_of(n, m), 0)),
        pl.BlockSpec((None, rows, tn),
                     lambda n, m: (layer, jnp.where(n < nt, m, mt - 1), jnp.minimum(n, nt - 1) + nb_off)),
    ]
    args = [x, w]
    if bias is not None:
        in_specs.append(pl.BlockSpec((None, 1, tn), lambda n, m: (layer, 0, col_of(n) + nb_off)))
        args.append(bias)
    out_shape = [jax.ShapeDtypeStruct((M, n_cols), d) for d in out_dtypes]
    out_specs = [pl.BlockSpec((tm, tn), lambda n, m: (row_of(n, m), col_of(n))) for _ in out_dtypes]
    out_bytes = sum(jnp.dtype(d).itemsize for d in out_dtypes)
    vmem = (2 * tm * K * 2 + 2 * K * tn * 2 + 2 * rows * tn * 4 + 2 * tm * tn * out_bytes
            + 2 * tm * tn * 4 + 4 * MIB)
    if side is not None:
        s_arr, s_idx = side
        R, C = s_arr.shape[1:]
        rs = R // (nt * mt)
        assert R % (nt * mt) == 0 and rs % BF16_SUBLANE == 0
        slab = lambda n, m: jnp.maximum((n - 1) * mt + m, 0)
        in_specs.append(pl.BlockSpec((None, rs, C), lambda n, m: (s_idx, slab(n, m), 0)))
        args.append(s_arr)
        out_shape.append(jax.ShapeDtypeStruct((R, C), BF16))
        out_specs.append(pl.BlockSpec((rs, C), lambda n, m: (slab(n, m), 0)))
        vmem += 2 * rs * C * (4 + 2)
    return pl.pallas_call(
        functools.partial(_mm_ws_kernel, act=act, has_bias=bias is not None, n_out=len(out_dtypes),
                          has_side=side is not None, rows=rows),
        out_shape=out_shape,
        grid=(nt + 1, mt),
        in_specs=in_specs,
        out_specs=out_specs,
        scratch_shapes=[pltpu.VMEM((2, K, tn), BF16)],
        compiler_params=_params(("arbitrary", "arbitrary"), vmem),
        name="matmul_" + act,
    )(*args)


def _resnorm_kernel(*refs, has_next):
    y_ref, m_ref, gpost_ref = refs[0], refs[1], refs[2]
    gnext_ref = refs[3] if has_next else None
    ynew_ref = refs[3 + has_next]
    ynew = y_ref[...] + _rms(m_ref[...], gpost_ref[...])
    ynew_ref[...] = ynew
    if has_next:
        refs[5][...] = _rms(ynew, gnext_ref[...]).astype(BF16)


def _resnorm(y, m, g_post, g_next=None, row_off=0, rows=None):
    D = y.shape[1]
    rows = y.shape[0] if rows is None else rows
    tm = _pick(rows, 256, BF16_SUBLANE)
    assert row_off % tm == 0
    ob = row_off // tm
    has_next = g_next is not None
    row = pl.BlockSpec((tm, D), lambda i: (i + ob, 0))
    out_row = pl.BlockSpec((tm, D), lambda i: (i, 0))
    vec = pl.BlockSpec((1, D), lambda i: (0, 0))
    out_shape = [jax.ShapeDtypeStruct((rows, D), F32)]
    if has_next:
        out_shape.append(jax.ShapeDtypeStruct((rows, D), BF16))
    vmem = 2 * tm * D * (4 + 4 + 4 + 2) + 4 * tm * D * 4 + 4 * MIB
    return pl.pallas_call(
        functools.partial(_resnorm_kernel, has_next=has_next),
        out_shape=out_shape,
        grid=(rows // tm,),
        in_specs=[row, row, vec] + ([vec] if has_next else []),
        out_specs=[out_row] * len(out_shape),
        compiler_params=_params(("parallel",), vmem),
        name="resnorm",
    )(*([y, m, g_post] + ([g_next] if has_next else [])))


def _pool_kernel(y_ref, hist_ref, gpre_ref, w_ref, scale_ref, gpost_ref, gnext_ref,
                 ynew_ref, hnext_ref, histout_ref, buf_ref, m_ref, lvl_a, lvl_b, *, nseq, tm, hist_valid):
    @pl.when(pl.program_id(0) < nseq)
    def _():
        _pool_tile(y_ref, hist_ref, gpre_ref, w_ref, scale_ref, gpost_ref, gnext_ref,
                   ynew_ref, hnext_ref, histout_ref, buf_ref, m_ref, lvl_a, lvl_b,
                   tm=tm, hist_valid=hist_valid)

    @pl.when(pl.program_id(0) >= nseq)
    def _():
        ynew_ref[...] = jnp.zeros(ynew_ref.shape, ynew_ref.dtype)
        hnext_ref[...] = jnp.zeros(hnext_ref.shape, hnext_ref.dtype)


def _pool_tile(y_ref, hist_ref, gpre_ref, w_ref, scale_ref, gpost_ref, gnext_ref,
               ynew_ref, hnext_ref, histout_ref, buf_ref, m_ref, lvl_a, lvl_b, *, tm, hist_valid):
    t = pl.program_id(1)
    D = y_ref.shape[-1]
    gw = D // len(POOL_WINDOWS)
    BASE = PAD + HALO
    n = HALO + tm
    for ref in (buf_ref, lvl_a, lvl_b):
        ref[0:PAD, :] = jnp.zeros((PAD, ref.shape[1]), F32)

    @pl.when(t == 0)
    def _():
        buf_ref[PAD:BASE, :] = hist_ref[...]

    @pl.when(t > 0)
    def _():
        buf_ref[PAD:BASE, :] = buf_ref[PAD + tm:BASE + tm, :]

    y = y_ref[...]
    h = _rms(y, gpre_ref[...])
    buf_ref[BASE:BASE + tm, :] = h
    histout_ref[...] = h[tm - HALO:, :]

    def window_sum(cols, w):
        read = lambda lo, rows: buf_ref[lo:lo + rows, cols]
        shift, dst, other = 1, lvl_a, lvl_b
        while 2 * shift < w:
            dst[PAD:PAD + n, :] = read(PAD, n) + read(PAD - shift, n)
            read = lambda lo, rows, src=dst: src[lo:lo + rows, :]
            shift, dst, other = 2 * shift, other, dst
        return read(BASE, tm) + read(BASE - shift, tm)

    pos = t * tm + lax.broadcasted_iota(jnp.int32, (tm, 1), 0)
    ssq = jnp.zeros((tm, 1), F32)
    for g, w in enumerate(POOL_WINDOWS):
        cols = slice(g * gw, (g + 1) * gw)
        hg = buf_ref[BASE:BASE + tm, cols]
        s = window_sum(cols, w)
        if hist_valid:
            mean = s / float(w)
        else:
            mean = s / jnp.minimum(pos + 1, w).astype(F32)
        d = mean - hg
        mg = jnp.dot(d.astype(BF16), w_ref[g], preferred_element_type=F32) * scale_ref[:, cols]
        m_ref[:, cols] = mg
        ssq = ssq + jnp.sum(mg * mg, axis=-1, keepdims=True)
    ynew = y + m_ref[...] * lax.rsqrt(ssq / D + NORM_EPS) * gpost_ref[...]
    ynew_ref[...] = ynew
    hnext_ref[...] = _rms(ynew, gnext_ref[...]).astype(BF16)


def _pool_segment(y, row_off, nseq, T, hist, hist_valid, out_rows, out_off, prev_outs,
                  g_pre, w_bf, scale, g_post, g_next):
    D = y.shape[1]
    G = len(POOL_WINDOWS)
    gw = D // G
    rest = 0 if prev_outs is not None else out_rows - out_off - nseq * T
    tm = _pick(math.gcd(T, rest), 256, BF16_SUBLANE)
    nt = T // tm
    assert row_off % tm == 0 and out_off % tm == 0
    ib, ob = row_off // tm, out_off // tm
    fill = rest // tm
    assert 0 <= fill <= nt
    tile_of = lambda s, t: jnp.minimum(s * nt + t, nseq * nt - 1)
    seq_of = lambda s: jnp.minimum(s, nseq - 1)
    vec = pl.BlockSpec((1, D), lambda s, t: (0, 0))
    in_specs = [
        pl.BlockSpec((tm, D), lambda s, t: (ib + tile_of(s, t), 0)),
        pl.BlockSpec((None, HALO, D), lambda s, t: (seq_of(s), 0, 0)),
        vec,
        pl.BlockSpec((G, gw, gw), lambda s, t: (0, 0, 0)),
        vec, vec, vec,
    ]
    args = [y, hist, g_pre, w_bf, scale, g_post, g_next]
    aliases = {}
    if prev_outs is not None:
        in_specs += [pl.BlockSpec(memory_space=pl.ANY)] * 2
        args += list(prev_outs)
        aliases = {7: 0, 8: 1}
    out_row = pl.BlockSpec((tm, D), lambda s, t: (ob + jnp.minimum(s * nt + t, nseq * nt + fill - 1), 0))
    vmem = (2 * tm * D * (4 + 4 + 2) + 2 * G * gw * gw * 2 + 2 * (tm + HALO) * D * 4
            + 3 * tm * D * 4 + 4 * MIB)
    kern = functools.partial(_pool_kernel, nseq=nseq, tm=tm, hist_valid=hist_valid)
    if prev_outs is not None:
        def kern(*refs, _k=kern):
            return _k(*refs[:7], *refs[9:])
    return pl.pallas_call(
        kern,
        out_shape=[jax.ShapeDtypeStruct((out_rows, D), F32),
                   jax.ShapeDtypeStruct((out_rows, D), BF16),
                   jax.ShapeDtypeStruct((nseq, HALO, D), F32)],
        grid=(nseq + (fill > 0), nt),
        in_specs=in_specs,
        out_specs=[out_row, out_row, pl.BlockSpec((None, HALO, D), lambda s, t: (seq_of(s), 0, 0))],
        scratch_shapes=[pltpu.VMEM((PAD + HALO + tm, D), F32), pltpu.VMEM((tm, D), F32),
                        pltpu.VMEM((PAD + HALO + tm, gw), F32), pltpu.VMEM((PAD + HALO + tm, gw), F32)],
        input_output_aliases=aliases,
        compiler_params=_params(("parallel", "arbitrary"), vmem),
        name="pool_mixer",
    )(*args)


def _sgu_gate_kernel(*refs, real, blk, nblk, groups, has_prev, emit_v):
    outs = refs[5 + has_prev:]

    @pl.when(pl.program_id(0) < real)
    def _():
        _sgu_gate_tile(*refs[:5], *outs, blk=blk, nblk=nblk, groups=groups, emit_v=emit_v)

    @pl.when(pl.program_id(0) >= real)
    def _():
        outs[0][...] = jnp.zeros(outs[0].shape, outs[0].dtype)


def _sgu_gate_tile(u_ref, v_ref, gv_ref, ws_ref, bst_ref, out_ref, *rest, blk, nblk, groups, emit_v):
    vout_ref = rest[0] if emit_v else None
    W = u_ref.shape[-1]
    gw = W // groups
    ri = lax.broadcasted_iota(jnp.int32, (SGU_BLOCK, SGU_BLOCK), 0)
    ci = lax.broadcasted_iota(jnp.int32, (SGU_BLOCK, SGU_BLOCK), 1)
    mask = (ci // CHUNK) <= (ri // CHUNK)
    wm = [jnp.where(mask, ws_ref[g], 0.0).astype(BF16)[:blk, :] for g in range(groups)]
    bias = [bst_ref[:blk, g:g + 1] for g in range(groups)]
    for b in range(nblk):
        rows = slice(b * blk, (b + 1) * blk)
        v = _rms(v_ref[rows, :], gv_ref[...])
        if emit_v:
            vout_ref[rows, :] = v
        vb = v.astype(BF16)
        if blk < SGU_BLOCK:
            vb = jnp.concatenate([vb, jnp.zeros((SGU_BLOCK - blk, W), BF16)], axis=0)
        for g in range(groups):
            cols = slice(g * gw, (g + 1) * gw)
            f = jnp.dot(wm[g], vb[:, cols], preferred_element_type=F32) + bias[g]
            out_ref[rows, cols] = (u_ref[rows, cols].astype(F32) * f).astype(BF16)


def _sgu_gate_segment(u, v_raw, row_off, rows, blk, g_v, w_s, b_s_t, prev_out, emit_v):
    M, W = u.shape
    groups = w_s.shape[0]
    rest = 0 if prev_out is not None else M - row_off - rows
    tm = _pick(math.gcd(rows, rest), 256, blk)
    nblk = tm // blk
    assert row_off % tm == 0
    ob = row_off // tm
    real, fill = rows // tm, rest // tm
    row = lambda i: (i + ob, 0)
    src = lambda i: (jnp.minimum(i, real - 1) + ob, 0)
    in_specs = [
        pl.BlockSpec((tm, W), src),
        pl.BlockSpec((tm, W), src),
        pl.BlockSpec((1, W), lambda i: (0, 0)),
        pl.BlockSpec((groups, SGU_BLOCK, SGU_BLOCK), lambda i: (0, 0, 0)),
        pl.BlockSpec((SGU_BLOCK, groups), lambda i: (0, 0)),
    ]
    args = [u, v_raw, g_v, w_s, b_s_t]
    aliases = {}
    if prev_out is not None:
        in_specs.append(pl.BlockSpec(memory_space=pl.ANY))
        args.append(prev_out)
        aliases = {5: 0}
    out_shape = [jax.ShapeDtypeStruct((M, W), BF16)]
    out_specs = [pl.BlockSpec((tm, W), row)]
    if emit_v:
        out_shape.append(jax.ShapeDtypeStruct((rows, W), F32))
        out_specs.append(pl.BlockSpec((tm, W), lambda i: (i, 0)))
    vmem = 2 * tm * W * (2 + 4 + 2 + 4) + 4 * tm * W * 4 + 4 * MIB
    return pl.pallas_call(
        functools.partial(_sgu_gate_kernel, real=real, blk=blk, nblk=nblk, groups=groups,
                          has_prev=prev_out is not None, emit_v=emit_v),
        out_shape=out_shape,
        grid=(real + fill,),
        in_specs=in_specs,
        out_specs=out_specs,
        input_output_aliases=aliases,
        compiler_params=_params(("parallel",), vmem),
        name="sgu_gate",
    )(*args)


def _sb_logs(z):
    ls = jnp.minimum(z, 0.0) - jnp.log(1.0 + jnp.exp(-jnp.abs(z)))
    return ls, ls - z


def _suffix_matrix(n):
    j = lax.broadcasted_iota(jnp.int32, (n, n), 0)
    s = lax.broadcasted_iota(jnp.int32, (n, n), 1)
    return jnp.where(j > s, 1.0, 0.0).astype(BF16)


def _suffix_sum(ln, U):
    n = ln.shape[0]
    hi = ln.astype(BF16)
    lo = (ln - hi.astype(F32)).astype(BF16)
    r = jnp.dot(jnp.concatenate([hi, lo], axis=0), U, preferred_element_type=F32)
    return r[:n] + r[n:]


def _sb_tiles(qs, ks, vs, cs, U, scale, allowed=None):
    dn = (((1,), (1,)), ((), ()))
    zs = [lax.dot_general(q, k, dn, preferred_element_type=F32) * scale for q, k in zip(qs, ks)]
    cw = U.shape[0]
    chunks = [slice(j * cw, (j + 1) * cw) for j in range(zs[0].shape[1] // cw)]
    logs = []
    for z in zs:
        ls, ln = _sb_logs(z)
        if allowed is not None:
            ln = jnp.where(allowed, ln, 0.0)
        logs.append((ls, [_suffix_sum(ln[:, cols], U) for cols in chunks],
                     [jnp.sum(ln[:, cols], axis=1, keepdims=True) for cols in chunks]))
    outs, new_cs = [], []
    for (ls, suffix, total), v, c in zip(logs, vs, cs):
        parts = [None] * len(chunks)
        for j in reversed(range(len(chunks))):
            parts[j] = jnp.exp(ls[:, chunks[j]] + suffix[j] + c)
            c = c + total[j]
        a = parts[0] if len(parts) == 1 else jnp.concatenate(parts, axis=1)
        if allowed is not None:
            a = jnp.where(allowed, a, 0.0)
        outs.append(jnp.dot(a.astype(BF16), v, preferred_element_type=F32))
        new_cs.append(c)
    return outs, new_cs


def _sb_prompt_kernel(q_ref, k_ref, v_ref, o_ref, u_ref, c_ref, acc_ref, *, nseq, tq, wide, heads, dh, scale):
    T = q_ref.shape[0]
    cols = [slice(g * dh, (g + 1) * dh) for g in range(heads)]

    def tiles(qs, ks, width, diagonal):
        allowed = None
        if diagonal:
            qi = lax.broadcasted_iota(jnp.int32, (tq, tq), 0)
            ki = lax.broadcasted_iota(jnp.int32, (tq, tq), 1)
            allowed = ki < qi
        outs, cs = _sb_tiles([q_ref[pl.ds(qs, tq), c] for c in cols],
                             [k_ref[pl.ds(ks, width * tq), c] for c in cols],
                             [v_ref[pl.ds(ks, width * tq), c] for c in cols],
                             [jnp.zeros((tq, 1), F32) if diagonal else c_ref[g] for g in range(heads)],
                             u_ref[...], scale, allowed)
        for g in range(heads):
            c_ref[g] = cs[g]
            acc_ref[g] = outs[g] if diagonal else acc_ref[g] + outs[g]

    def q_body(i, carry):
        qs = pl.multiple_of(i * tq, tq)
        tiles(qs, qs, 1, True)

        def wide_body(p, carry):
            tiles(qs, pl.multiple_of((i - wide * (p + 1)) * tq, tq), wide, False)
            return carry

        lax.fori_loop(0, i // wide, wide_body, 0)

        def single_body(jj, carry):
            tiles(qs, pl.multiple_of((i % wide - 1 - jj) * tq, tq), 1, False)
            return carry

        lax.fori_loop(0, i % wide, single_body, 0)
        for g in range(heads):
            o_ref[pl.ds(qs, tq), cols[g]] = acc_ref[g].astype(o_ref.dtype)
        return carry

    @pl.when(pl.program_id(0) < nseq)
    def _():
        u_ref[...] = _suffix_matrix(tq)
        lax.fori_loop(0, T // tq, q_body, 0)

    @pl.when(pl.program_id(0) >= nseq)
    def _():
        o_ref[...] = jnp.zeros(o_ref.shape, o_ref.dtype)


def _sb_prompt(q, k, v, B, T, H, dh, out_rows):
    tq = _pick(T, SB_QBLOCK, LANE)
    heads = _pick(H, SB_PROMPT_HEADS, 1)
    nb = pl.cdiv(out_rows, T)
    src = pl.BlockSpec((T, heads * dh),
                       lambda b, h: (jnp.minimum(b, B - 1), jnp.where(b < B, h, H // heads - 1)))
    vmem = 2 * 4 * T * heads * dh * 2 + heads * SB_WIDE * 24 * tq * tq * 4 + 4 * MIB
    return pl.pallas_call(
        functools.partial(_sb_prompt_kernel, nseq=B, tq=tq, wide=SB_WIDE, heads=heads, dh=dh,
                          scale=dh ** -0.5),
        out_shape=jax.ShapeDtypeStruct((out_rows, H * dh), BF16),
        grid=(nb, H // heads),
        in_specs=[src, src, src],
        out_specs=pl.BlockSpec((T, heads * dh), lambda b, h: (b, h)),
        scratch_shapes=[pltpu.VMEM((tq, tq), BF16), pltpu.VMEM((heads, tq, 1), F32),
                        pltpu.VMEM((heads, tq, dh), F32)],
        compiler_params=_params(("parallel", "parallel"), vmem),
        name="sb_prompt",
    )(q, k, v)


def _sb_sample_kernel(q_ref, kn_ref, vn_ref, kc_ref, vc_ref, prev_ref, o_ref, u_ref, c_ref, acc_ref,
                      *, heads, scale):
    del prev_ref
    pc = pl.program_id(2)
    Ts = q_ref.shape[0]
    dh = kc_ref.shape[2]
    k_all = pltpu.einshape("phd->hpd", kc_ref[...]).astype(BF16)
    v_all = pltpu.einshape("phd->hpd", vc_ref[...]).astype(BF16)
    cols = [slice(g * dh, (g + 1) * dh) for g in range(heads)]

    @pl.when(pc == 0)
    def _():
        u_ref[...] = _suffix_matrix(u_ref.shape[0])
        qi = lax.broadcasted_iota(jnp.int32, (Ts, Ts), 0)
        ki = lax.broadcasted_iota(jnp.int32, (Ts, Ts), 1)
        outs, cs = _sb_tiles([q_ref[:, c] for c in cols], [kn_ref[:, c] for c in cols],
                             [vn_ref[:, c] for c in cols], [jnp.zeros((Ts, 1), F32)] * heads,
                             _suffix_matrix(Ts), scale, ki < qi)
        for g in range(heads):
            c_ref[g] = cs[g]
            acc_ref[g] = outs[g]

    for g0 in range(0, heads, SB_SAMPLE_BATCH):
        gs = range(g0, g0 + SB_SAMPLE_BATCH)
        outs, cs = _sb_tiles([q_ref[:, cols[g]] for g in gs], [k_all[g] for g in gs],
                             [v_all[g] for g in gs], [c_ref[g] for g in gs], u_ref[...], scale)
        for g, o, c in zip(gs, outs, cs):
            c_ref[g] = c
            acc_ref[g] = acc_ref[g] + o

    @pl.when(pc == pl.num_programs(2) - 1)
    def _():
        for g in range(heads):
            o_ref[:, g * dh:(g + 1) * dh] = acc_ref[g].astype(o_ref.dtype)


def _sb_sample(q, k, v, cache_k, cache_v, layer, row_off, Bs, Ts, H, dh, prev_out):
    P = cache_k.shape[2]
    heads = F32_SUBLANE
    assert H % heads == 0 and row_off % Ts == 0
    cw = _pick(P, SB_CUM, LANE)
    tp = _pick(P, SB_SAMPLE_KEYS, cw)
    npc = P // tp
    ob = row_off // Ts
    new = pl.BlockSpec((Ts, heads * dh), lambda b, g, p: (ob + b, g))
    old = pl.BlockSpec((None, None, tp, heads, dh), lambda b, g, p: (layer, b, npc - 1 - p, g, 0))
    vmem = 2 * 2 * tp * heads * dh * 4 + heads * 16 * Ts * tp * 4 + 8 * MIB
    return pl.pallas_call(
        functools.partial(_sb_sample_kernel, heads=heads, scale=dh ** -0.5),
        out_shape=jax.ShapeDtypeStruct(prev_out.shape, BF16),
        grid=(Bs, H // heads, npc),
        in_specs=[new, new, new, old, old, pl.BlockSpec(memory_space=pl.ANY)],
        out_specs=new,
        scratch_shapes=[pltpu.VMEM((cw, cw), BF16), pltpu.VMEM((heads, Ts, 1), F32),
                        pltpu.VMEM((heads, Ts, dh), F32)],
        input_output_aliases={5: 0},
        compiler_params=_params(("parallel", "parallel", "arbitrary"), vmem),
        name="sb_sample",
    )(q, k, v, cache_k, cache_v, prev_out)


def kernel(x_prompt, x_sample, cache_pool, cache_k, cache_v, g_mix_pre, g_mix_post, g_ffn_pre, g_ffn_post, pool_w, pool_scale, sgu_w_in, sgu_b_in, sgu_g_v, sgu_w_s, sgu_b_s, sgu_w_out, sb_w_qkv, sb_w_o, ffn_w_up, ffn_w_down):
    B, T, D = x_prompt.shape
    Bs, Ts, _ = x_sample.shape
    Mp, Ms = B * T, Bs * Ts
    M = Mp + Ms
    depth = g_mix_pre.shape[0]
    H, dh = cache_k.shape[3], cache_k.shape[4]
    P = cache_k.shape[2]
    keep = min(T, P)
    W = sgu_g_v.shape[1]

    vec = lambda a, i: a[i].reshape(1, -1)
    pool_w_bf = pool_w.astype(BF16)

    zero_hist = jnp.zeros((B, HALO, D), F32)
    pad_hist = lambda c: jnp.pad(c, ((0, 0), (HALO - POOL_HIST, 0), (0, 0)))

    y = None
    h = None
    pool_hist_p, pool_hist_s, sgu_v_s = [], [], []
    sb_k_p, sb_v_p, sb_k_s, sb_v_s = [], [], [], []
    for i in range(depth):
        kind, j = i % 3, i // 3
        g_ffn = vec(g_ffn_pre, i)
        if kind == 0:
            if y is None:
                src_p, off_p, src_s, off_s = x_prompt.reshape(Mp, D), 0, x_sample.reshape(Ms, D), 0
            else:
                src_p, off_p, src_s, off_s = y, 0, y, Mp
            common = (vec(g_mix_pre, i), pool_w_bf[j], vec(pool_scale, j), vec(g_mix_post, i), g_ffn)
            y_p, h_p, hist_p = _pool_segment(src_p, off_p, B, T, zero_hist, False, M, 0, None, *common)
            y, h, hist_s = _pool_segment(src_s, off_s, Bs, Ts, pad_hist(cache_pool[j]), True, M, Mp,
                                         (y_p, h_p), *common)
            pool_hist_p.append(hist_p[:, HALO - POOL_HIST:])
            pool_hist_s.append(hist_s[:, HALO - POOL_HIST:])
        else:
            if kind == 1:
                b_in = sgu_b_in.reshape(sgu_b_in.shape[0], 1, -1)
                (u,) = _matmul(h, sgu_w_in, j, 0, W, [BF16], act="gelu", bias=b_in)
                (v_raw,) = _matmul(h, sgu_w_in, j, W, W, [F32], act="gelu", bias=b_in)
                gate_args = (vec(sgu_g_v, j), sgu_w_s[j], sgu_b_s[j].T)
                (gated,) = _sgu_gate_segment(u, v_raw, 0, Mp, min(T, SGU_BLOCK), *gate_args, None, False)
                gated, v_s = _sgu_gate_segment(u, v_raw, Mp, Ms, min(Ts, SGU_BLOCK), *gate_args, gated, True)
                sgu_v_s.append(v_s.reshape(Bs, Ts, W))
                (m,) = _matmul(gated, sgu_w_out, j, 0, D, [F32])
            else:
                (q,) = _matmul(h, sb_w_qkv, j, 0, D, [BF16])
                k32, k = _matmul(h, sb_w_qkv, j, D, D, [F32, BF16])
                v32, v = _matmul(h, sb_w_qkv, j, 2 * D, D, [F32, BF16])
                o = _sb_prompt(q, k, v, B, T, H, dh, M)
                o = _sb_sample(q, k, v, cache_k, cache_v, j, Mp, Bs, Ts, H, dh, o)
                (m,) = _matmul(o, sb_w_o, j, 0, D, [F32])
                sb_k_p.append(k32[:Mp].reshape(B, T, H, dh)[:, -keep:])
                sb_v_p.append(v32[:Mp].reshape(B, T, H, dh)[:, -keep:])
                sb_k_s.append(k32[Mp:].reshape(Bs, Ts, H, dh))
                sb_v_s.append(v32[Mp:].reshape(Bs, Ts, H, dh))
            y, h = _resnorm(y, m, vec(g_mix_post, i), g_ffn)
        a, w_down_bf = _matmul(h, ffn_w_up, i, 0, ffn_w_up.shape[2], [BF16], act="relu2",
                               side=(ffn_w_down, i))
        m = _matmul_kgrid(a, w_down_bf, F32)
        g_post = vec(g_ffn_post, i)
        if i + 1 == depth:
            (yp,) = _resnorm(y, m, g_post, None, 0, Mp)
            (ys,) = _resnorm(y, m, g_post, None, Mp, Ms)
        elif (i + 1) % 3 == 0:
            (y,) = _resnorm(y, m, g_post)
        else:
            y, h = _resnorm(y, m, g_post, vec(g_mix_pre, i + 1))
    return (yp.reshape(B, T, D), ys.reshape(Bs, Ts, D),
            jnp.stack(pool_hist_p), jnp.stack(pool_hist_s), jnp.stack(sgu_v_s),
            jnp.stack(sb_k_p), jnp.stack(sb_v_p), jnp.stack(sb_k_s), jnp.stack(sb_v_s))
```

```python
import functools
import math

import jax
import jax.numpy as jnp
from jax import lax
from jax.experimental import pallas as pl
from jax.experimental.pallas import tpu as pltpu

F32 = jnp.float32
BF16 = jnp.bfloat16

NORM_EPS = 1e-6
CHUNK = 64
POOL_WINDOWS = (2, 4, 8, 16)
POOL_HIST = 15
HALO = 16
PAD = 8
SGU_BLOCK = 128
ROW_TILE = 256
MM_ROWS = 1088
MM_COLS = 1024
MM_DEPTH = 4096
SB_QBLOCK = 256
SB_CUM = 256
SB_PROMPT_HEADS = 4
SB_WIDE = 4
SB_SAMPLE_KEYS = 1024
SB_SAMPLE_BATCH = 4

LANE = 128
F32_SUBLANE = 8
BF16_SUBLANE = 16
VMEM_CAP = 60 * 1024 * 1024
MIB = 1024 * 1024


def _pick(n, cap, mult):
    best = None
    for d in range(mult, min(n, cap) + 1, mult):
        if n % d == 0:
            best = d
    assert best is not None, (n, cap, mult)
    return best


def _params(sem, vmem_bytes):
    return pltpu.CompilerParams(dimension_semantics=sem,
                                vmem_limit_bytes=int(min(VMEM_CAP, vmem_bytes)))


def _rms(x, g):
    ms = jnp.mean(x * x, axis=-1, keepdims=True)
    return x * lax.rsqrt(ms + NORM_EPS) * g


def _activate(acc, act):
    if act == "relu2":
        r = jnp.maximum(acc, 0.0)
        return r * r
    if act == "gelu":
        return 0.5 * acc * (1.0 + lax.erf(acc * (2.0 ** -0.5)))
    return acc


def _mm_kernel(x_ref, w_ref, o_ref, acc_ref, *, nk):
    k = pl.program_id(2)
    dot = lambda: jnp.dot(x_ref[...], w_ref[...], preferred_element_type=F32)
    if nk == 1:
        o_ref[...] = dot().astype(o_ref.dtype)
        return

    @pl.when(k == 0)
    def _():
        acc_ref[...] = dot()

    @pl.when(jnp.logical_and(k > 0, k < nk - 1))
    def _():
        acc_ref[...] += dot()

    @pl.when(k == nk - 1)
    def _():
        o_ref[...] = (acc_ref[...] + dot()).astype(o_ref.dtype)


def _matmul_kgrid(x, w, out_dtype):
    M, K = x.shape
    N = w.shape[1]
    tm = _pick(M, MM_ROWS, BF16_SUBLANE)
    tn = _pick(N, MM_COLS, LANE)
    tk = _pick(K, MM_DEPTH, LANE)
    nk = K // tk
    vmem = (2 * tm * tk * 2 + 2 * tk * tn * 2 + 2 * tm * tn * jnp.dtype(out_dtype).itemsize
            + 3 * tm * tn * 4 + 4 * MIB)
    return pl.pallas_call(
        functools.partial(_mm_kernel, nk=nk),
        out_shape=jax.ShapeDtypeStruct((M, N), out_dtype),
        grid=(N // tn, M // tm, nk),
        in_specs=[pl.BlockSpec((tm, tk), lambda n, m, k: (m, k)),
                  pl.BlockSpec((tk, tn), lambda n, m, k: (k, n))],
        out_specs=pl.BlockSpec((tm, tn), lambda n, m, k: (m, n)),
        scratch_shapes=[pltpu.VMEM((tm, tn), F32)],
        compiler_params=_params(("parallel", "parallel", "arbitrary"), vmem),
        name="matmul_kgrid",
    )(x, w)


def _mm_ws_kernel(*refs, act, has_bias, n_out, has_side, rows):
    x_ref, w_ref = refs[0], refs[1]
    pos = 2
    b_ref = refs[pos] if has_bias else None
    pos += has_bias
    side_in = refs[pos] if has_side else None
    pos += has_side
    outs = refs[pos:pos + n_out]
    pos += n_out
    side_out = refs[pos] if has_side else None
    pos += has_side
    wbf_ref = refs[pos]
    n, m = pl.program_id(0), pl.program_id(1)
    last = pl.num_programs(0) - 1

    def multiply():
        acc = jnp.dot(x_ref[...], wbf_ref[(n - 1) % 2], preferred_element_type=F32)
        if has_bias:
            acc = acc + b_ref[...]
        acc = _activate(acc, act)
        for o in outs:
            o[...] = acc.astype(o.dtype)
        if has_side:
            side_out[...] = side_in[...].astype(side_out.dtype)

    def cast_next():
        r0 = pl.multiple_of(m * rows, rows)
        wbf_ref[n % 2, pl.ds(r0, rows), :] = w_ref[...].astype(BF16)

    @pl.when(n == 0)
    def _():
        cast_next()

    @pl.when(jnp.logical_and(n > 0, n < last))
    def _():
        multiply()
        cast_next()

    @pl.when(n == last)
    def _():
        multiply()


def _matmul(x, w, layer, n_off, n_cols, out_dtypes, act="none", bias=None, side=None):
    M, K = x.shape
    tm = _pick(M, MM_ROWS, BF16_SUBLANE)
    tn = _pick(n_cols, MM_COLS, LANE)
    mt, nt = M // tm, n_cols // tn
    rows = K // mt
    assert K <= MM_DEPTH and K % mt == 0 and rows % BF16_SUBLANE == 0 and n_off % tn == 0
    nb_off = n_off // tn
    row_of = lambda n, m: jnp.where(n > 0, m, 0)
    col_of = lambda n: jnp.maximum(n - 1, 0)
    in_specs = [
        pl.BlockSpec((tm, K), lambda n, m: (row_of(n, m), 0)),
        pl.BlockSpec((None, rows, tn),
                     lambda n, m: (layer, jnp.where(n < nt, m, mt - 1), jnp.minimum(n, nt - 1) + nb_off)),
    ]
    args = [x, w]
    if bias is not None:
        in_specs.append(pl.BlockSpec((None, 1, tn), lambda n, m: (layer, 0, col_of(n) + nb_off)))
        args.append(bias)
    out_shape = [jax.ShapeDtypeStruct((M, n_cols), d) for d in out_dtypes]
    out_specs = [pl.BlockSpec((tm, tn), lambda n, m: (row_of(n, m), col_of(n))) for _ in out_dtypes]
    out_bytes = sum(jnp.dtype(d).itemsize for d in out_dtypes)
    vmem = (2 * tm * K * 2 + 2 * K * tn * 2 + 2 * rows * tn * 4 + 2 * tm * tn * out_bytes
            + 2 * tm * tn * 4 + 4 * MIB)
    if side is not None:
        s_arr, s_idx = side
        R, C = s_arr.shape[1:]
        rs = R // (nt * mt)
        assert R % (nt * mt) == 0 and rs % BF16_SUBLANE == 0
        slab = lambda n, m: jnp.maximum((n - 1) * mt + m, 0)
        in_specs.append(pl.BlockSpec((None, rs, C), lambda n, m: (s_idx, slab(n, m), 0)))
        args.append(s_arr)
        out_shape.append(jax.ShapeDtypeStruct((R, C), BF16))
        out_specs.append(pl.BlockSpec((rs, C), lambda n, m: (slab(n, m), 0)))
        vmem += 2 * rs * C * (4 + 2)
    return pl.pallas_call(
        functools.partial(_mm_ws_kernel, act=act, has_bias=bias is not None, n_out=len(out_dtypes),
                          has_side=side is not None, rows=rows),
        out_shape=out_shape,
        grid=(nt + 1, mt),
        in_specs=in_specs,
        out_specs=out_specs,
        scratch_shapes=[pltpu.VMEM((2, K, tn), BF16)],
        compiler_params=_params(("arbitrary", "arbitrary"), vmem),
        name="matmul_" + act,
    )(*args)


def _resnorm_kernel(*refs, has_next):
    y_ref, m_ref, gpost_ref = refs[0], refs[1], refs[2]
    gnext_ref = refs[3] if has_next else None
    ynew_ref = refs[3 + has_next]
    ynew = y_ref[...] + _rms(m_ref[...], gpost_ref[...])
    ynew_ref[...] = ynew
    if has_next:
        refs[5][...] = _rms(ynew, gnext_ref[...]).astype(BF16)


def _resnorm(y, m, g_post, g_next=None, row_off=0, rows=None):
    D = y.shape[1]
    rows = y.shape[0] if rows is None else rows
    tm = _pick(rows, ROW_TILE, BF16_SUBLANE)
    assert row_off % tm == 0
    ob = row_off // tm
    has_next = g_next is not None
    row = pl.BlockSpec((tm, D), lambda i: (i + ob, 0))
    out_row = pl.BlockSpec((tm, D), lambda i: (i, 0))
    vec = pl.BlockSpec((1, D), lambda i: (0, 0))
    out_shape = [jax.ShapeDtypeStruct((rows, D), F32)]
    if has_next:
        out_shape.append(jax.ShapeDtypeStruct((rows, D), BF16))
    vmem = 2 * tm * D * (4 + 4 + 4 + 2) + 4 * tm * D * 4 + 4 * MIB
    return pl.pallas_call(
        functools.partial(_resnorm_kernel, has_next=has_next),
        out_shape=out_shape,
        grid=(rows // tm,),
        in_specs=[row, row, vec] + ([vec] if has_next else []),
        out_specs=[out_row] * len(out_shape),
        compiler_params=_params(("parallel",), vmem),
        name="resnorm",
    )(*([y, m, g_post] + ([g_next] if has_next else [])))


def _pool_kernel(y_ref, hist_ref, gpre_ref, w_ref, scale_ref, gpost_ref, gnext_ref,
                 ynew_ref, hnext_ref, histout_ref, buf_ref, m_ref, lvl_a, lvl_b, *, nseq, tm, hist_valid):
    @pl.when(pl.program_id(0) < nseq)
    def _():
        _pool_tile(y_ref, hist_ref, gpre_ref, w_ref, scale_ref, gpost_ref, gnext_ref,
                   ynew_ref, hnext_ref, histout_ref, buf_ref, m_ref, lvl_a, lvl_b,
                   tm=tm, hist_valid=hist_valid)

    @pl.when(pl.program_id(0) >= nseq)
    def _():
        ynew_ref[...] = jnp.zeros(ynew_ref.shape, ynew_ref.dtype)
        hnext_ref[...] = jnp.zeros(hnext_ref.shape, hnext_ref.dtype)


def _pool_tile(y_ref, hist_ref, gpre_ref, w_ref, scale_ref, gpost_ref, gnext_ref,
               ynew_ref, hnext_ref, histout_ref, buf_ref, m_ref, lvl_a, lvl_b, *, tm, hist_valid):
    t = pl.program_id(1)
    D = y_ref.shape[-1]
    gw = D // len(POOL_WINDOWS)
    BASE = PAD + HALO
    n = HALO + tm
    for ref in (buf_ref, lvl_a, lvl_b):
        ref[0:PAD, :] = jnp.zeros((PAD, ref.shape[1]), F32)

    @pl.when(t == 0)
    def _():
        buf_ref[PAD:BASE, :] = hist_ref[...]

    @pl.when(t > 0)
    def _():
        buf_ref[PAD:BASE, :] = buf_ref[PAD + tm:BASE + tm, :]

    y = y_ref[...]
    h = _rms(y, gpre_ref[...])
    buf_ref[BASE:BASE + tm, :] = h
    histout_ref[...] = h[tm - HALO:, :]

    def window_sum(cols, w):
        read = lambda lo, rows: buf_ref[lo:lo + rows, cols]
        shift, dst, other = 1, lvl_a, lvl_b
        while 2 * shift < w:
            dst[PAD:PAD + n, :] = read(PAD, n) + read(PAD - shift, n)
            read = lambda lo, rows, src=dst: src[lo:lo + rows, :]
            shift, dst, other = 2 * shift, other, dst
        return read(BASE, tm) + read(BASE - shift, tm)

    pos = t * tm + lax.broadcasted_iota(jnp.int32, (tm, 1), 0)
    ssq = jnp.zeros((tm, 1), F32)
    for g, w in enumerate(POOL_WINDOWS):
        cols = slice(g * gw, (g + 1) * gw)
        hg = buf_ref[BASE:BASE + tm, cols]
        s = window_sum(cols, w)
        if hist_valid:
            mean = s / float(w)
        else:
            mean = s / jnp.minimum(pos + 1, w).astype(F32)
        d = mean - hg
        mg = jnp.dot(d.astype(BF16), w_ref[g], preferred_element_type=F32) * scale_ref[:, cols]
        m_ref[:, cols] = mg
        ssq = ssq + jnp.sum(mg * mg, axis=-1, keepdims=True)
    ynew = y + m_ref[...] * lax.rsqrt(ssq / D + NORM_EPS) * gpost_ref[...]
    ynew_ref[...] = ynew
    hnext_ref[...] = _rms(ynew, gnext_ref[...]).astype(BF16)


def _pool_segment(y, row_off, nseq, T, hist, hist_valid, out_rows, out_off, prev_outs,
                  g_pre, w_bf, scale, g_post, g_next):
    D = y.shape[1]
    G = len(POOL_WINDOWS)
    gw = D // G
    rest = 0 if prev_outs is not None else out_rows - out_off - nseq * T
    tm = _pick(math.gcd(T, rest), ROW_TILE, BF16_SUBLANE)
    nt = T // tm
    assert row_off % tm == 0 and out_off % tm == 0
    ib, ob = row_off // tm, out_off // tm
    fill = rest // tm
    assert 0 <= fill <= nt
    tile_of = lambda s, t: jnp.minimum(s * nt + t, nseq * nt - 1)
    seq_of = lambda s: jnp.minimum(s, nseq - 1)
    vec = pl.BlockSpec((1, D), lambda s, t: (0, 0))
    in_specs = [
        pl.BlockSpec((tm, D), lambda s, t: (ib + tile_of(s, t), 0)),
        pl.BlockSpec((None, HALO, D), lambda s, t: (seq_of(s), 0, 0)),
        vec,
        pl.BlockSpec((G, gw, gw), lambda s, t: (0, 0, 0)),
        vec, vec, vec,
    ]
    args = [y, hist, g_pre, w_bf, scale, g_post, g_next]
    aliases = {}
    if prev_outs is not None:
        in_specs += [pl.BlockSpec(memory_space=pl.ANY)] * 2
        args += list(prev_outs)
        aliases = {7: 0, 8: 1}
    out_row = pl.BlockSpec((tm, D), lambda s, t: (ob + jnp.minimum(s * nt + t, nseq * nt + fill - 1), 0))
    vmem = (2 * tm * D * (4 + 4 + 2) + 2 * G * gw * gw * 2 + 2 * (tm + HALO) * D * 4
            + 3 * tm * D * 4 + 4 * MIB)
    kern = functools.partial(_pool_kernel, nseq=nseq, tm=tm, hist_valid=hist_valid)
    if prev_outs is not None:
        def kern(*refs, _k=kern):
            return _k(*refs[:7], *refs[9:])
    return pl.pallas_call(
        kern,
        out_shape=[jax.ShapeDtypeStruct((out_rows, D), F32),
                   jax.ShapeDtypeStruct((out_rows, D), BF16),
                   jax.ShapeDtypeStruct((nseq, HALO, D), F32)],
        grid=(nseq + (fill > 0), nt),
        in_specs=in_specs,
        out_specs=[out_row, out_row, pl.BlockSpec((None, HALO, D), lambda s, t: (seq_of(s), 0, 0))],
        scratch_shapes=[pltpu.VMEM((PAD + HALO + tm, D), F32), pltpu.VMEM((tm, D), F32),
                        pltpu.VMEM((PAD + HALO + tm, gw), F32), pltpu.VMEM((PAD + HALO + tm, gw), F32)],
        input_output_aliases=aliases,
        compiler_params=_params(("parallel", "arbitrary"), vmem),
        name="pool_mixer",
    )(*args)


def _sgu_gate_kernel(*refs, real, blk, nblk, groups, has_prev, emit_v):
    outs = refs[5 + has_prev:]

    @pl.when(pl.program_id(0) < real)
    def _():
        _sgu_gate_tile(*refs[:5], *outs, blk=blk, nblk=nblk, groups=groups, emit_v=emit_v)

    @pl.when(pl.program_id(0) >= real)
    def _():
        outs[0][...] = jnp.zeros(outs[0].shape, outs[0].dtype)


def _sgu_gate_tile(u_ref, v_ref, gv_ref, ws_ref, bst_ref, out_ref, *rest, blk, nblk, groups, emit_v):
    vout_ref = rest[0] if emit_v else None
    W = u_ref.shape[-1]
    gw = W // groups
    ri = lax.broadcasted_iota(jnp.int32, (SGU_BLOCK, SGU_BLOCK), 0)
    ci = lax.broadcasted_iota(jnp.int32, (SGU_BLOCK, SGU_BLOCK), 1)
    mask = (ci // CHUNK) <= (ri // CHUNK)
    wm = [jnp.where(mask, ws_ref[g], 0.0).astype(BF16)[:blk, :] for g in range(groups)]
    bias = [bst_ref[:blk, g:g + 1] for g in range(groups)]
    for b in range(nblk):
        rows = slice(b * blk, (b + 1) * blk)
        v = _rms(v_ref[rows, :], gv_ref[...])
        if emit_v:
            vout_ref[rows, :] = v
        vb = v.astype(BF16)
        if blk < SGU_BLOCK:
            vb = jnp.concatenate([vb, jnp.zeros((SGU_BLOCK - blk, W), BF16)], axis=0)
        for g in range(groups):
            cols = slice(g * gw, (g + 1) * gw)
            f = jnp.dot(wm[g], vb[:, cols], preferred_element_type=F32) + bias[g]
            out_ref[rows, cols] = (u_ref[rows, cols].astype(F32) * f).astype(BF16)


def _sgu_gate_segment(u, v_raw, row_off, rows, blk, g_v, w_s, b_s_t, prev_out, emit_v):
    M, W = u.shape
    groups = w_s.shape[0]
    rest = 0 if prev_out is not None else M - row_off - rows
    tm = _pick(math.gcd(rows, rest), ROW_TILE, blk)
    nblk = tm // blk
    assert row_off % tm == 0
    ob = row_off // tm
    real, fill = rows // tm, rest // tm
    row = lambda i: (i + ob, 0)
    src = lambda i: (jnp.minimum(i, real - 1) + ob, 0)
    in_specs = [
        pl.BlockSpec((tm, W), src),
        pl.BlockSpec((tm, W), src),
        pl.BlockSpec((1, W), lambda i: (0, 0)),
        pl.BlockSpec((groups, SGU_BLOCK, SGU_BLOCK), lambda i: (0, 0, 0)),
        pl.BlockSpec((SGU_BLOCK, groups), lambda i: (0, 0)),
    ]
    args = [u, v_raw, g_v, w_s, b_s_t]
    aliases = {}
    if prev_out is not None:
        in_specs.append(pl.BlockSpec(memory_space=pl.ANY))
        args.append(prev_out)
        aliases = {5: 0}
    out_shape = [jax.ShapeDtypeStruct((M, W), BF16)]
    out_specs = [pl.BlockSpec((tm, W), row)]
    if emit_v:
        out_shape.append(jax.ShapeDtypeStruct((rows, W), F32))
        out_specs.append(pl.BlockSpec((tm, W), lambda i: (i, 0)))
    vmem = 2 * tm * W * (2 + 4 + 2 + 4) + 4 * tm * W * 4 + 4 * MIB
    return pl.pallas_call(
        functools.partial(_sgu_gate_kernel, real=real, blk=blk, nblk=nblk, groups=groups,
                          has_prev=prev_out is not None, emit_v=emit_v),
        out_shape=out_shape,
        grid=(real + fill,),
        in_specs=in_specs,
        out_specs=out_specs,
        input_output_aliases=aliases,
        compiler_params=_params(("parallel",), vmem),
        name="sgu_gate",
    )(*args)


def _sb_logs(z):
    ls = jnp.minimum(z, 0.0) - jnp.log(1.0 + jnp.exp(-jnp.abs(z)))
    return ls, ls - z


def _suffix_matrix(n):
    j = lax.broadcasted_iota(jnp.int32, (n, n), 0)
    s = lax.broadcasted_iota(jnp.int32, (n, n), 1)
    return jnp.where(j > s, 1.0, 0.0).astype(BF16)


def _suffix_sum(ln, U):
    n = ln.shape[0]
    hi = ln.astype(BF16)
    lo = (ln - hi.astype(F32)).astype(BF16)
    r = jnp.dot(jnp.concatenate([hi, lo], axis=0), U, preferred_element_type=F32)
    return r[:n] + r[n:]


def _sb_tiles(qs, ks, vs, cs, U, scale, allowed=None):
    dn = (((1,), (1,)), ((), ()))
    zs = [lax.dot_general(q, k, dn, preferred_element_type=F32) * scale for q, k in zip(qs, ks)]
    cw = U.shape[0]
    chunks = [slice(j * cw, (j + 1) * cw) for j in range(zs[0].shape[1] // cw)]
    logs = []
    for z in zs:
        ls, ln = _sb_logs(z)
        if allowed is not None:
            ln = jnp.where(allowed, ln, 0.0)
        logs.append((ls, [_suffix_sum(ln[:, cols], U) for cols in chunks],
                     [jnp.sum(ln[:, cols], axis=1, keepdims=True) for cols in chunks]))
    outs, new_cs = [], []
    for (ls, suffix, total), v, c in zip(logs, vs, cs):
        parts = [None] * len(chunks)
        for j in reversed(range(len(chunks))):
            parts[j] = jnp.exp(ls[:, chunks[j]] + suffix[j] + c)
            c = c + total[j]
        a = parts[0] if len(parts) == 1 else jnp.concatenate(parts, axis=1)
        if allowed is not None:
            a = jnp.where(allowed, a, 0.0)
        outs.append(jnp.dot(a.astype(BF16), v, preferred_element_type=F32))
        new_cs.append(c)
    return outs, new_cs


def _sb_prompt_kernel(q_ref, k_ref, v_ref, o_ref, u_ref, c_ref, acc_ref, *, nseq, tq, wide, heads, dh, scale):
    T = q_ref.shape[0]
    cols = [slice(g * dh, (g + 1) * dh) for g in range(heads)]

    def tiles(qs, ks, width, diagonal):
        allowed = None
        if diagonal:
            qi = lax.broadcasted_iota(jnp.int32, (tq, tq), 0)
            ki = lax.broadcasted_iota(jnp.int32, (tq, tq), 1)
            allowed = ki < qi
        outs, cs = _sb_tiles([q_ref[pl.ds(qs, tq), c] for c in cols],
                             [k_ref[pl.ds(ks, width * tq), c] for c in cols],
                             [v_ref[pl.ds(ks, width * tq), c] for c in cols],
                             [jnp.zeros((tq, 1), F32) if diagonal else c_ref[g] for g in range(heads)],
                             u_ref[...], scale, allowed)
        for g in range(heads):
            c_ref[g] = cs[g]
            acc_ref[g] = outs[g] if diagonal else acc_ref[g] + outs[g]

    def q_body(i, carry):
        qs = pl.multiple_of(i * tq, tq)
        tiles(qs, qs, 1, True)

        def wide_body(p, carry):
            tiles(qs, pl.multiple_of((i - wide * (p + 1)) * tq, tq), wide, False)
            return carry

        lax.fori_loop(0, i // wide, wide_body, 0)

        def single_body(jj, carry):
            tiles(qs, pl.multiple_of((i % wide - 1 - jj) * tq, tq), 1, False)
            return carry

        lax.fori_loop(0, i % wide, single_body, 0)
        for g in range(heads):
            o_ref[pl.ds(qs, tq), cols[g]] = acc_ref[g].astype(o_ref.dtype)
        return carry

    @pl.when(pl.program_id(0) < nseq)
    def _():
        u_ref[...] = _suffix_matrix(tq)
        lax.fori_loop(0, T // tq, q_body, 0)

    @pl.when(pl.program_id(0) >= nseq)
    def _():
        o_ref[...] = jnp.zeros(o_ref.shape, o_ref.dtype)


def _sb_prompt(q, k, v, B, T, H, dh, out_rows):
    tq = _pick(T, SB_QBLOCK, LANE)
    heads = _pick(H, SB_PROMPT_HEADS, 1)
    nb = pl.cdiv(out_rows, T)
    src = pl.BlockSpec((T, heads * dh),
                       lambda b, h: (jnp.minimum(b, B - 1), jnp.where(b < B, h, H // heads - 1)))
    vmem = 2 * 4 * T * heads * dh * 2 + heads * SB_WIDE * 24 * tq * tq * 4 + 4 * MIB
    return pl.pallas_call(
        functools.partial(_sb_prompt_kernel, nseq=B, tq=tq, wide=SB_WIDE, heads=heads, dh=dh,
                          scale=dh ** -0.5),
        out_shape=jax.ShapeDtypeStruct((out_rows, H * dh), BF16),
        grid=(nb, H // heads),
        in_specs=[src, src, src],
        out_specs=pl.BlockSpec((T, heads * dh), lambda b, h: (b, h)),
        scratch_shapes=[pltpu.VMEM((tq, tq), BF16), pltpu.VMEM((heads, tq, 1), F32),
                        pltpu.VMEM((heads, tq, dh), F32)],
        compiler_params=_params(("parallel", "parallel"), vmem),
        name="sb_prompt",
    )(q, k, v)


def _sb_sample_kernel(q_ref, kn_ref, vn_ref, kc_ref, vc_ref, prev_ref, o_ref, u_ref, c_ref, acc_ref,
                      *, heads, scale):
    del prev_ref
    pc = pl.program_id(2)
    Ts = q_ref.shape[0]
    dh = kc_ref.shape[2]
    k_all = pltpu.einshape("phd->hpd", kc_ref[...]).astype(BF16)
    v_all = pltpu.einshape("phd->hpd", vc_ref[...]).astype(BF16)
    cols = [slice(g * dh, (g + 1) * dh) for g in range(heads)]

    @pl.when(pc == 0)
    def _():
        u_ref[...] = _suffix_matrix(u_ref.shape[0])
        qi = lax.broadcasted_iota(jnp.int32, (Ts, Ts), 0)
        ki = lax.broadcasted_iota(jnp.int32, (Ts, Ts), 1)
        outs, cs = _sb_tiles([q_ref[:, c] for c in cols], [kn_ref[:, c] for c in cols],
                             [vn_ref[:, c] for c in cols], [jnp.zeros((Ts, 1), F32)] * heads,
                             _suffix_matrix(Ts), scale, ki < qi)
        for g in range(heads):
            c_ref[g] = cs[g]
            acc_ref[g] = outs[g]

    for g0 in range(0, heads, SB_SAMPLE_BATCH):
        gs = range(g0, g0 + SB_SAMPLE_BATCH)
        outs, cs = _sb_tiles([q_ref[:, cols[g]] for g in gs], [k_all[g] for g in gs],
                             [v_all[g] for g in gs], [c_ref[g] for g in gs], u_ref[...], scale)
        for g, o, c in zip(gs, outs, cs):
            c_ref[g] = c
            acc_ref[g] = acc_ref[g] + o

    @pl.when(pc == pl.num_programs(2) - 1)
    def _():
        for g in range(heads):
            o_ref[:, g * dh:(g + 1) * dh] = acc_ref[g].astype(o_ref.dtype)


def _sb_sample(q, k, v, cache_k, cache_v, layer, row_off, Bs, Ts, H, dh, prev_out):
    P = cache_k.shape[2]
    heads = F32_SUBLANE
    assert H % heads == 0 and row_off % Ts == 0
    cw = _pick(P, SB_CUM, LANE)
    tp = _pick(P, SB_SAMPLE_KEYS, cw)
    npc = P // tp
    ob = row_off // Ts
    new = pl.BlockSpec((Ts, heads * dh), lambda b, g, p: (ob + b, g))
    old = pl.BlockSpec((None, None, tp, heads, dh), lambda b, g, p: (layer, b, npc - 1 - p, g, 0))
    vmem = 2 * 2 * tp * heads * dh * 4 + heads * 16 * Ts * tp * 4 + 8 * MIB
    return pl.pallas_call(
        functools.partial(_sb_sample_kernel, heads=heads, scale=dh ** -0.5),
        out_shape=jax.ShapeDtypeStruct(prev_out.shape, BF16),
        grid=(Bs, H // heads, npc),
        in_specs=[new, new, new, old, old, pl.BlockSpec(memory_space=pl.ANY)],
        out_specs=new,
        scratch_shapes=[pltpu.VMEM((cw, cw), BF16), pltpu.VMEM((heads, Ts, 1), F32),
                        pltpu.VMEM((heads, Ts, dh), F32)],
        input_output_aliases={5: 0},
        compiler_params=_params(("parallel", "parallel", "arbitrary"), vmem),
        name="sb_sample",
    )(q, k, v, cache_k, cache_v, prev_out)


def kernel(x_prompt, x_sample, cache_pool, cache_k, cache_v, g_mix_pre, g_mix_post, g_ffn_pre, g_ffn_post, pool_w, pool_scale, sgu_w_in, sgu_b_in, sgu_g_v, sgu_w_s, sgu_b_s, sgu_w_out, sb_w_qkv, sb_w_o, ffn_w_up, ffn_w_down):
    B, T, D = x_prompt.shape
    Bs, Ts, _ = x_sample.shape
    Mp, Ms = B * T, Bs * Ts
    M = Mp + Ms
    depth = g_mix_pre.shape[0]
    H, dh = cache_k.shape[3], cache_k.shape[4]
    P = cache_k.shape[2]
    keep = min(T, P)
    W = sgu_g_v.shape[1]

    vec = lambda a, i: a[i].reshape(1, -1)
    pool_w_bf = pool_w.astype(BF16)

    zero_hist = jnp.zeros((B, HALO, D), F32)
    pad_hist = lambda c: jnp.pad(c, ((0, 0), (HALO - POOL_HIST, 0), (0, 0)))

    y = None
    h = None
    pool_hist_p, pool_hist_s, sgu_v_s = [], [], []
    sb_k_p, sb_v_p, sb_k_s, sb_v_s = [], [], [], []
    for i in range(depth):
        kind, j = i % 3, i // 3
        g_ffn = vec(g_ffn_pre, i)
        if kind == 0:
            if y is None:
                src_p, off_p, src_s, off_s = x_prompt.reshape(Mp, D), 0, x_sample.reshape(Ms, D), 0
            else:
                src_p, off_p, src_s, off_s = y, 0, y, Mp
            common = (vec(g_mix_pre, i), pool_w_bf[j], vec(pool_scale, j), vec(g_mix_post, i), g_ffn)
            y_p, h_p, hist_p = _pool_segment(src_p, off_p, B, T, zero_hist, False, M, 0, None, *common)
            y, h, hist_s = _pool_segment(src_s, off_s, Bs, Ts, pad_hist(cache_pool[j]), True, M, Mp,
                                         (y_p, h_p), *common)
            pool_hist_p.append(hist_p[:, HALO - POOL_HIST:])
            pool_hist_s.append(hist_s[:, HALO - POOL_HIST:])
        else:
            if kind == 1:
                b_in = sgu_b_in.reshape(sgu_b_in.shape[0], 1, -1)
                (u,) = _matmul(h, sgu_w_in, j, 0, W, [BF16], act="gelu", bias=b_in)
                (v_raw,) = _matmul(h, sgu_w_in, j, W, W, [F32], act="gelu", bias=b_in)
                gate_args = (vec(sgu_g_v, j), sgu_w_s[j], sgu_b_s[j].T)
                (gated,) = _sgu_gate_segment(u, v_raw, 0, Mp, min(T, SGU_BLOCK), *gate_args, None, False)
                gated, v_s = _sgu_gate_segment(u, v_raw, Mp, Ms, min(Ts, SGU_BLOCK), *gate_args, gated, True)
                sgu_v_s.append(v_s.reshape(Bs, Ts, W))
                (m,) = _matmul(gated, sgu_w_out, j, 0, D, [F32])
            else:
                (q,) = _matmul(h, sb_w_qkv, j, 0, D, [BF16])
                k32, k = _matmul(h, sb_w_qkv, j, D, D, [F32, BF16])
                v32, v = _matmul(h, sb_w_qkv, j, 2 * D, D, [F32, BF16])
                o = _sb_prompt(q, k, v, B, T, H, dh, M)
                o = _sb_sample(q, k, v, cache_k, cache_v, j, Mp, Bs, Ts, H, dh, o)
                (m,) = _matmul(o, sb_w_o, j, 0, D, [F32])
                sb_k_p.append(k32[:Mp].reshape(B, T, H, dh)[:, -keep:])
                sb_v_p.append(v32[:Mp].reshape(B, T, H, dh)[:, -keep:])
                sb_k_s.append(k32[Mp:].reshape(Bs, Ts, H, dh))
                sb_v_s.append(v32[Mp:].reshape(Bs, Ts, H, dh))
            y, h = _resnorm(y, m, vec(g_mix_post, i), g_ffn)
        a, w_down_bf = _matmul(h, ffn_w_up, i, 0, ffn_w_up.shape[2], [BF16], act="relu2",
                               side=(ffn_w_down, i))
        m = _matmul_kgrid(a, w_down_bf, F32)
        g_post = vec(g_ffn_post, i)
        if i + 1 == depth:
            (yp,) = _resnorm(y, m, g_post, None, 0, Mp)
            (ys,) = _resnorm(y, m, g_post, None, Mp, Ms)
        elif (i + 1) % 3 == 0:
            (y,) = _resnorm(y, m, g_post)
        else:
            y, h = _resnorm(y, m, g_post, vec(g_mix_pre, i + 1))
    return (yp.reshape(B, T, D), ys.reshape(Bs, Ts, D),
            jnp.stack(pool_hist_p), jnp.stack(pool_hist_s), jnp.stack(sgu_v_s),
            jnp.stack(sb_k_p), jnp.stack(sb_v_p), jnp.stack(sb_k_s), jnp.stack(sb_v_s))
```

```python
import functools
import math

import jax
import jax.numpy as jnp
from jax import lax
from jax.experimental import pallas as pl
from jax.experimental.pallas import tpu as pltpu

F32 = jnp.float32
BF16 = jnp.bfloat16

NORM_EPS = 1e-6
CHUNK = 64
POOL_WINDOWS = (2, 4, 8, 16)
POOL_HIST = 15
HALO = 16
PAD = 8
SGU_BLOCK = 128
ROW_TILE = 256
MM_ROWS = 1088
MM_COLS = 1024
MM_DEPTH = 4096
SB_QBLOCK = 256
SB_CUM = 256
SB_PROMPT_HEADS = 4
SB_WIDE = 4
SB_SAMPLE_KEYS = 1024
SB_SAMPLE_BATCH = 8

LANE = 128
F32_SUBLANE = 8
BF16_SUBLANE = 16
VMEM_CAP = 60 * 1024 * 1024
MIB = 1024 * 1024


def _pick(n, cap, mult):
    best = None
    for d in range(mult, min(n, cap) + 1, mult):
        if n % d == 0:
            best = d
    assert best is not None, (n, cap, mult)
    return best


def _params(sem, vmem_bytes):
    return pltpu.CompilerParams(dimension_semantics=sem,
                                vmem_limit_bytes=int(min(VMEM_CAP, vmem_bytes)))


def _rms(x, g):
    ms = jnp.mean(x * x, axis=-1, keepdims=True)
    return x * lax.rsqrt(ms + NORM_EPS) * g


def _activate(acc, act):
    if act == "relu2":
        r = jnp.maximum(acc, 0.0)
        return r * r
    if act == "gelu":
        return 0.5 * acc * (1.0 + lax.erf(acc * (2.0 ** -0.5)))
    return acc


def _mm_kernel(x_ref, w_ref, o_ref, acc_ref, *, nk):
    k = pl.program_id(2)
    dot = lambda: jnp.dot(x_ref[...], w_ref[...], preferred_element_type=F32)
    if nk == 1:
        o_ref[...] = dot().astype(o_ref.dtype)
        return

    @pl.when(k == 0)
    def _():
        acc_ref[...] = dot()

    @pl.when(jnp.logical_and(k > 0, k < nk - 1))
    def _():
        acc_ref[...] += dot()

    @pl.when(k == nk - 1)
    def _():
        o_ref[...] = (acc_ref[...] + dot()).astype(o_ref.dtype)


def _matmul_kgrid(x, w, out_dtype):
    M, K = x.shape
    N = w.shape[1]
    tm = _pick(M, MM_ROWS, BF16_SUBLANE)
    tn = _pick(N, MM_COLS, LANE)
    tk = _pick(K, MM_DEPTH, LANE)
    nk = K // tk
    vmem = (2 * tm * tk * 2 + 2 * tk * tn * 2 + 2 * tm * tn * jnp.dtype(out_dtype).itemsize
            + 3 * tm * tn * 4 + 4 * MIB)
    return pl.pallas_call(
        functools.partial(_mm_kernel, nk=nk),
        out_shape=jax.ShapeDtypeStruct((M, N), out_dtype),
        grid=(N // tn, M // tm, nk),
        in_specs=[pl.BlockSpec((tm, tk), lambda n, m, k: (m, k)),
                  pl.BlockSpec((tk, tn), lambda n, m, k: (k, n))],
        out_specs=pl.BlockSpec((tm, tn), lambda n, m, k: (m, n)),
        scratch_shapes=[pltpu.VMEM((tm, tn), F32)],
        compiler_params=_params(("parallel", "parallel", "arbitrary"), vmem),
        name="matmul_kgrid",
    )(x, w)


def _mm_ws_kernel(*refs, act, has_bias, n_out, has_side, rows):
    x_ref, w_ref = refs[0], refs[1]
    pos = 2
    b_ref = refs[pos] if has_bias else None
    pos += has_bias
    side_in = refs[pos] if has_side else None
    pos += has_side
    outs = refs[pos:pos + n_out]
    pos += n_out
    side_out = refs[pos] if has_side else None
    pos += has_side
    wbf_ref = refs[pos]
    n, m = pl.program_id(0), pl.program_id(1)
    last = pl.num_programs(0) - 1

    def multiply():
        acc = jnp.dot(x_ref[...], wbf_ref[(n - 1) % 2], preferred_element_type=F32)
        if has_bias:
            acc = acc + b_ref[...]
        acc = _activate(acc, act)
        for o in outs:
            o[...] = acc.astype(o.dtype)
        if has_side:
            side_out[...] = side_in[...].astype(side_out.dtype)

    def cast_next():
        r0 = pl.multiple_of(m * rows, rows)
        wbf_ref[n % 2, pl.ds(r0, rows), :] = w_ref[...].astype(BF16)

    @pl.when(n == 0)
    def _():
        cast_next()

    @pl.when(jnp.logical_and(n > 0, n < last))
    def _():
        multiply()
        cast_next()

    @pl.when(n == last)
    def _():
        multiply()


def _matmul(x, w, layer, n_off, n_cols, out_dtypes, act="none", bias=None, side=None):
    M, K = x.shape
    tm = _pick(M, MM_ROWS, BF16_SUBLANE)
    tn = _pick(n_cols, MM_COLS, LANE)
    mt, nt = M // tm, n_cols // tn
    rows = K // mt
    assert K <= MM_DEPTH and K % mt == 0 and rows % BF16_SUBLANE == 0 and n_off % tn == 0
    nb_off = n_off // tn
    row_of = lambda n, m: jnp.where(n > 0, m, 0)
    col_of = lambda n: jnp.maximum(n - 1, 0)
    in_specs = [
        pl.BlockSpec((tm, K), lambda n, m: (row_of(n, m), 0)),
        pl.BlockSpec((None, rows, tn),
                     lambda n, m: (layer, jnp.where(n < nt, m, mt - 1), jnp.minimum(n, nt - 1) + nb_off)),
    ]
    args = [x, w]
    if bias is not None:
        in_specs.append(pl.BlockSpec((None, 1, tn), lambda n, m: (layer, 0, col_of(n) + nb_off)))
        args.append(bias)
    out_shape = [jax.ShapeDtypeStruct((M, n_cols), d) for d in out_dtypes]
    out_specs = [pl.BlockSpec((tm, tn), lambda n, m: (row_of(n, m), col_of(n))) for _ in out_dtypes]
    out_bytes = sum(jnp.dtype(d).itemsize for d in out_dtypes)
    vmem = (2 * tm * K * 2 + 2 * K * tn * 2 + 2 * rows * tn * 4 + 2 * tm * tn * out_bytes
            + 2 * tm * tn * 4 + 4 * MIB)
    if side is not None:
        s_arr, s_idx = side
        R, C = s_arr.shape[1:]
        rs = R // (nt * mt)
        assert R % (nt * mt) == 0 and rs % BF16_SUBLANE == 0
        slab = lambda n, m: jnp.maximum((n - 1) * mt + m, 0)
        in_specs.append(pl.BlockSpec((None, rs, C), lambda n, m: (s_idx, slab(n, m), 0)))
        args.append(s_arr)
        out_shape.append(jax.ShapeDtypeStruct((R, C), BF16))
        out_specs.append(pl.BlockSpec((rs, C), lambda n, m: (slab(n, m), 0)))
        vmem += 2 * rs * C * (4 + 2)
    return pl.pallas_call(
        functools.partial(_mm_ws_kernel, act=act, has_bias=bias is not None, n_out=len(out_dtypes),
                          has_side=side is not None, rows=rows),
        out_shape=out_shape,
        grid=(nt + 1, mt),
        in_specs=in_specs,
        out_specs=out_specs,
        scratch_shapes=[pltpu.VMEM((2, K, tn), BF16)],
        compiler_params=_params(("arbitrary", "arbitrary"), vmem),
        name="matmul_" + act,
    )(*args)


def _resnorm_kernel(*refs, has_next):
    y_ref, m_ref, gpost_ref = refs[0], refs[1], refs[2]
    gnext_ref = refs[3] if has_next else None
    ynew_ref = refs[3 + has_next]
    ynew = y_ref[...] + _rms(m_ref[...], gpost_ref[...])
    ynew_ref[...] = ynew
    if has_next:
        refs[5][...] = _rms(ynew, gnext_ref[...]).astype(BF16)


def _resnorm(y, m, g_post, g_next=None, row_off=0, rows=None):
    D = y.shape[1]
    rows = y.shape[0] if rows is None else rows
    tm = _pick(rows, ROW_TILE, BF16_SUBLANE)
    assert row_off % tm == 0
    ob = row_off // tm
    has_next = g_next is not None
    row = pl.BlockSpec((tm, D), lambda i: (i + ob, 0))
    out_row = pl.BlockSpec((tm, D), lambda i: (i, 0))
    vec = pl.BlockSpec((1, D), lambda i: (0, 0))
    out_shape = [jax.ShapeDtypeStruct((rows, D), F32)]
    if has_next:
        out_shape.append(jax.ShapeDtypeStruct((rows, D), BF16))
    vmem = 2 * tm * D * (4 + 4 + 4 + 2) + 4 * tm * D * 4 + 4 * MIB
    return pl.pallas_call(
        functools.partial(_resnorm_kernel, has_next=has_next),
        out_shape=out_shape,
        grid=(rows // tm,),
        in_specs=[row, row, vec] + ([vec] if has_next else []),
        out_specs=[out_row] * len(out_shape),
        compiler_params=_params(("parallel",), vmem),
        name="resnorm",
    )(*([y, m, g_post] + ([g_next] if has_next else [])))


def _pool_kernel(y_ref, hist_ref, gpre_ref, w_ref, scale_ref, gpost_ref, gnext_ref,
                 ynew_ref, hnext_ref, histout_ref, buf_ref, m_ref, lvl_a, lvl_b, *, nseq, tm, hist_valid):
    @pl.when(pl.program_id(0) < nseq)
    def _():
        _pool_tile(y_ref, hist_ref, gpre_ref, w_ref, scale_ref, gpost_ref, gnext_ref,
                   ynew_ref, hnext_ref, histout_ref, buf_ref, m_ref, lvl_a, lvl_b,
                   tm=tm, hist_valid=hist_valid)

    @pl.when(pl.program_id(0) >= nseq)
    def _():
        ynew_ref[...] = jnp.zeros(ynew_ref.shape, ynew_ref.dtype)
        hnext_ref[...] = jnp.zeros(hnext_ref.shape, hnext_ref.dtype)


def _pool_tile(y_ref, hist_ref, gpre_ref, w_ref, scale_ref, gpost_ref, gnext_ref,
               ynew_ref, hnext_ref, histout_ref, buf_ref, m_ref, lvl_a, lvl_b, *, tm, hist_valid):
    t = pl.program_id(1)
    D = y_ref.shape[-1]
    gw = D // len(POOL_WINDOWS)
    BASE = PAD + HALO
    n = HALO + tm
    for ref in (buf_ref, lvl_a, lvl_b):
        ref[0:PAD, :] = jnp.zeros((PAD, ref.shape[1]), F32)

    @pl.when(t == 0)
    def _():
        buf_ref[PAD:BASE, :] = hist_ref[...]

    @pl.when(t > 0)
    def _():
        buf_ref[PAD:BASE, :] = buf_ref[PAD + tm:BASE + tm, :]

    y = y_ref[...]
    h = _rms(y, gpre_ref[...])
    buf_ref[BASE:BASE + tm, :] = h
    histout_ref[...] = h[tm - HALO:, :]

    def window_sum(cols, w):
        read = lambda lo, rows: buf_ref[lo:lo + rows, cols]
        shift, dst, other = 1, lvl_a, lvl_b
        while 2 * shift < w:
            dst[PAD:PAD + n, :] = read(PAD, n) + read(PAD - shift, n)
            read = lambda lo, rows, src=dst: src[lo:lo + rows, :]
            shift, dst, other = 2 * shift, other, dst
        return read(BASE, tm) + read(BASE - shift, tm)

    pos = t * tm + lax.broadcasted_iota(jnp.int32, (tm, 1), 0)
    ssq = jnp.zeros((tm, 1), F32)
    for g, w in enumerate(POOL_WINDOWS):
        cols = slice(g * gw, (g + 1) * gw)
        hg = buf_ref[BASE:BASE + tm, cols]
        s = window_sum(cols, w)
        if hist_valid:
            mean = s / float(w)
        else:
            mean = s / jnp.minimum(pos + 1, w).astype(F32)
        d = mean - hg
        mg = jnp.dot(d.astype(BF16), w_ref[g], preferred_element_type=F32) * scale_ref[:, cols]
        m_ref[:, cols] = mg
        ssq = ssq + jnp.sum(mg * mg, axis=-1, keepdims=True)
    ynew = y + m_ref[...] * lax.rsqrt(ssq / D + NORM_EPS) * gpost_ref[...]
    ynew_ref[...] = ynew
    hnext_ref[...] = _rms(ynew, gnext_ref[...]).astype(BF16)


def _pool_segment(y, row_off, nseq, T, hist, hist_valid, out_rows, out_off, prev_outs,
                  g_pre, w_bf, scale, g_post, g_next):
    D = y.shape[1]
    G = len(POOL_WINDOWS)
    gw = D // G
    rest = 0 if prev_outs is not None else out_rows - out_off - nseq * T
    tm = _pick(math.gcd(T, rest), ROW_TILE, BF16_SUBLANE)
    nt = T // tm
    assert row_off % tm == 0 and out_off % tm == 0
    ib, ob = row_off // tm, out_off // tm
    fill = rest // tm
    assert 0 <= fill <= nt
    tile_of = lambda s, t: jnp.minimum(s * nt + t, nseq * nt - 1)
    seq_of = lambda s: jnp.minimum(s, nseq - 1)
    vec = pl.BlockSpec((1, D), lambda s, t: (0, 0))
    in_specs = [
        pl.BlockSpec((tm, D), lambda s, t: (ib + tile_of(s, t), 0)),
        pl.BlockSpec((None, HALO, D), lambda s, t: (seq_of(s), 0, 0)),
        vec,
        pl.BlockSpec((G, gw, gw), lambda s, t: (0, 0, 0)),
        vec, vec, vec,
    ]
    args = [y, hist, g_pre, w_bf, scale, g_post, g_next]
    aliases = {}
    if prev_outs is not None:
        in_specs += [pl.BlockSpec(memory_space=pl.ANY)] * 2
        args += list(prev_outs)
        aliases = {7: 0, 8: 1}
    out_row = pl.BlockSpec((tm, D), lambda s, t: (ob + jnp.minimum(s * nt + t, nseq * nt + fill - 1), 0))
    vmem = (2 * tm * D * (4 + 4 + 2) + 2 * G * gw * gw * 2 + 2 * (tm + HALO) * D * 4
            + 3 * tm * D * 4 + 4 * MIB)
    kern = functools.partial(_pool_kernel, nseq=nseq, tm=tm, hist_valid=hist_valid)
    if prev_outs is not None:
        def kern(*refs, _k=kern):
            return _k(*refs[:7], *refs[9:])
    return pl.pallas_call(
        kern,
        out_shape=[jax.ShapeDtypeStruct((out_rows, D), F32),
                   jax.ShapeDtypeStruct((out_rows, D), BF16),
                   jax.ShapeDtypeStruct((nseq, HALO, D), F32)],
        grid=(nseq + (fill > 0), nt),
        in_specs=in_specs,
        out_specs=[out_row, out_row, pl.BlockSpec((None, HALO, D), lambda s, t: (seq_of(s), 0, 0))],
        scratch_shapes=[pltpu.VMEM((PAD + HALO + tm, D), F32), pltpu.VMEM((tm, D), F32),
                        pltpu.VMEM((PAD + HALO + tm, gw), F32), pltpu.VMEM((PAD + HALO + tm, gw), F32)],
        input_output_aliases=aliases,
        compiler_params=_params(("parallel", "arbitrary"), vmem),
        name="pool_mixer",
    )(*args)


def _sgu_gate_kernel(*refs, real, blk, nblk, groups, has_prev, emit_v):
    outs = refs[5 + has_prev:]

    @pl.when(pl.program_id(0) < real)
    def _():
        _sgu_gate_tile(*refs[:5], *outs, blk=blk, nblk=nblk, groups=groups, emit_v=emit_v)

    @pl.when(pl.program_id(0) >= real)
    def _():
        outs[0][...] = jnp.zeros(outs[0].shape, outs[0].dtype)


def _sgu_gate_tile(u_ref, v_ref, gv_ref, ws_ref, bst_ref, out_ref, *rest, blk, nblk, groups, emit_v):
    vout_ref = rest[0] if emit_v else None
    W = u_ref.shape[-1]
    gw = W // groups
    ri = lax.broadcasted_iota(jnp.int32, (SGU_BLOCK, SGU_BLOCK), 0)
    ci = lax.broadcasted_iota(jnp.int32, (SGU_BLOCK, SGU_BLOCK), 1)
    mask = (ci // CHUNK) <= (ri // CHUNK)
    wm = [jnp.where(mask, ws_ref[g], 0.0).astype(BF16)[:blk, :] for g in range(groups)]
    bias = [bst_ref[:blk, g:g + 1] for g in range(groups)]
    for b in range(nblk):
        rows = slice(b * blk, (b + 1) * blk)
        v = _rms(v_ref[rows, :], gv_ref[...])
        if emit_v:
            vout_ref[rows, :] = v
        vb = v.astype(BF16)
        if blk < SGU_BLOCK:
            vb = jnp.concatenate([vb, jnp.zeros((SGU_BLOCK - blk, W), BF16)], axis=0)
        for g in range(groups):
            cols = slice(g * gw, (g + 1) * gw)
            f = jnp.dot(wm[g], vb[:, cols], preferred_element_type=F32) + bias[g]
            out_ref[rows, cols] = (u_ref[rows, cols].astype(F32) * f).astype(BF16)


def _sgu_gate_segment(u, v_raw, row_off, rows, blk, g_v, w_s, b_s_t, prev_out, emit_v):
    M, W = u.shape
    groups = w_s.shape[0]
    rest = 0 if prev_out is not None else M - row_off - rows
    tm = _pick(math.gcd(rows, rest), ROW_TILE, blk)
    nblk = tm // blk
    assert row_off % tm == 0
    ob = row_off // tm
    real, fill = rows // tm, rest // tm
    row = lambda i: (i + ob, 0)
    src = lambda i: (jnp.minimum(i, real - 1) + ob, 0)
    in_specs = [
        pl.BlockSpec((tm, W), src),
        pl.BlockSpec((tm, W), src),
        pl.BlockSpec((1, W), lambda i: (0, 0)),
        pl.BlockSpec((groups, SGU_BLOCK, SGU_BLOCK), lambda i: (0, 0, 0)),
        pl.BlockSpec((SGU_BLOCK, groups), lambda i: (0, 0)),
    ]
    args = [u, v_raw, g_v, w_s, b_s_t]
    aliases = {}
    if prev_out is not None:
        in_specs.append(pl.BlockSpec(memory_space=pl.ANY))
        args.append(prev_out)
        aliases = {5: 0}
    out_shape = [jax.ShapeDtypeStruct((M, W), BF16)]
    out_specs = [pl.BlockSpec((tm, W), row)]
    if emit_v:
        out_shape.append(jax.ShapeDtypeStruct((rows, W), F32))
        out_specs.append(pl.BlockSpec((tm, W), lambda i: (i, 0)))
    vmem = 2 * tm * W * (2 + 4 + 2 + 4) + 4 * tm * W * 4 + 4 * MIB
    return pl.pallas_call(
        functools.partial(_sgu_gate_kernel, real=real, blk=blk, nblk=nblk, groups=groups,
                          has_prev=prev_out is not None, emit_v=emit_v),
        out_shape=out_shape,
        grid=(real + fill,),
        in_specs=in_specs,
        out_specs=out_specs,
        input_output_aliases=aliases,
        compiler_params=_params(("parallel",), vmem),
        name="sgu_gate",
    )(*args)


def _sb_logs(z):
    ls = jnp.minimum(z, 0.0) - jnp.log(1.0 + jnp.exp(-jnp.abs(z)))
    return ls, ls - z


def _suffix_matrix(n):
    j = lax.broadcasted_iota(jnp.int32, (n, n), 0)
    s = lax.broadcasted_iota(jnp.int32, (n, n), 1)
    return jnp.where(j > s, 1.0, 0.0).astype(BF16)


def _suffix_sum(ln, U):
    n = ln.shape[0]
    hi = ln.astype(BF16)
    lo = (ln - hi.astype(F32)).astype(BF16)
    r = jnp.dot(jnp.concatenate([hi, lo], axis=0), U, preferred_element_type=F32)
    return r[:n] + r[n:]


def _sb_tiles(qs, ks, vs, cs, U, scale, allowed=None):
    dn = (((1,), (1,)), ((), ()))
    zs = [lax.dot_general(q, k, dn, preferred_element_type=F32) * scale for q, k in zip(qs, ks)]
    cw = U.shape[0]
    chunks = [slice(j * cw, (j + 1) * cw) for j in range(zs[0].shape[1] // cw)]
    logs = []
    for z in zs:
        ls, ln = _sb_logs(z)
        if allowed is not None:
            ln = jnp.where(allowed, ln, 0.0)
        logs.append((ls, [_suffix_sum(ln[:, cols], U) for cols in chunks],
                     [jnp.sum(ln[:, cols], axis=1, keepdims=True) for cols in chunks]))
    outs, new_cs = [], []
    for (ls, suffix, total), v, c in zip(logs, vs, cs):
        parts = [None] * len(chunks)
        for j in reversed(range(len(chunks))):
            parts[j] = jnp.exp(ls[:, chunks[j]] + suffix[j] + c)
            c = c + total[j]
        a = parts[0] if len(parts) == 1 else jnp.concatenate(parts, axis=1)
        if allowed is not None:
            a = jnp.where(allowed, a, 0.0)
        outs.append(jnp.dot(a.astype(BF16), v, preferred_element_type=F32))
        new_cs.append(c)
    return outs, new_cs


def _sb_prompt_kernel(q_ref, k_ref, v_ref, o_ref, u_ref, c_ref, acc_ref,
                      *, nseq, tq, wide, heads, dh, scale):
    T = q_ref.shape[0]
    cols = [slice(g * dh, (g + 1) * dh) for g in range(heads)]

    def tiles(qs, ks, width, diagonal):
        allowed = None
        if diagonal:
            qi = lax.broadcasted_iota(jnp.int32, (tq, tq), 0)
            ki = lax.broadcasted_iota(jnp.int32, (tq, tq), 1)
            allowed = ki < qi
        outs, cs = _sb_tiles([q_ref[pl.ds(qs, tq), c] for c in cols],
                             [k_ref[pl.ds(ks, width * tq), c] for c in cols],
                             [v_ref[pl.ds(ks, width * tq), c] for c in cols],
                             [jnp.zeros((tq, 1), F32) if diagonal else c_ref[g] for g in range(heads)],
                             u_ref[...], scale, allowed)
        for g in range(heads):
            c_ref[g] = cs[g]
            acc_ref[g] = outs[g] if diagonal else acc_ref[g] + outs[g]

    def q_body(i, carry):
        qs = pl.multiple_of(i * tq, tq)
        tiles(qs, qs, 1, True)

        def wide_body(p, carry):
            tiles(qs, pl.multiple_of((i - wide * (p + 1)) * tq, tq), wide, False)
            return carry

        lax.fori_loop(0, i // wide, wide_body, 0)

        def single_body(jj, carry):
            tiles(qs, pl.multiple_of((i % wide - 1 - jj) * tq, tq), 1, False)
            return carry

        lax.fori_loop(0, i % wide, single_body, 0)
        for g in range(heads):
            o_ref[pl.ds(qs, tq), cols[g]] = acc_ref[g].astype(o_ref.dtype)
        return carry

    @pl.when(pl.program_id(0) < nseq)
    def _():
        u_ref[...] = _suffix_matrix(tq)
        lax.fori_loop(0, T // tq, q_body, 0)

    @pl.when(pl.program_id(0) >= nseq)
    def _():
        o_ref[...] = jnp.zeros(o_ref.shape, o_ref.dtype)


def _sb_prompt(q, k, v, B, T, H, dh, out_rows):
    tq = _pick(T, SB_QBLOCK, LANE)
    heads = _pick(H, SB_PROMPT_HEADS, 1)
    nb = pl.cdiv(out_rows, T)
    src = pl.BlockSpec((T, heads * dh),
                       lambda b, h: (jnp.minimum(b, B - 1), jnp.where(b < B, h, H // heads - 1)))
    vmem = 2 * 4 * T * heads * dh * 2 + heads * SB_WIDE * 24 * tq * tq * 4 + 4 * MIB
    return pl.pallas_call(
        functools.partial(_sb_prompt_kernel, nseq=B, tq=tq, wide=SB_WIDE, heads=heads, dh=dh,
                          scale=dh ** -0.5),
        out_shape=jax.ShapeDtypeStruct((out_rows, H * dh), BF16),
        grid=(nb, H // heads),
        in_specs=[src, src, src],
        out_specs=pl.BlockSpec((T, heads * dh), lambda b, h: (b, h)),
        scratch_shapes=[pltpu.VMEM((tq, tq), BF16), pltpu.VMEM((heads, tq, 1), F32),
                        pltpu.VMEM((heads, tq, dh), F32)],
        compiler_params=_params(("parallel", "parallel"), vmem),
        name="sb_prompt",
    )(q, k, v)


def _sb_sample_kernel(q_ref, kn_ref, vn_ref, kc_ref, vc_ref, prev_ref, o_ref, u_ref, c_ref, acc_ref,
                      k_all, v_all, *, heads, scale):
    del prev_ref
    pc = pl.program_id(2)
    Ts = q_ref.shape[0]
    dh = kc_ref.shape[2]
    k_all[...] = pltpu.einshape("phd->hpd", kc_ref[...].astype(BF16))
    v_all[...] = pltpu.einshape("phd->hpd", vc_ref[...].astype(BF16))
    cols = [slice(g * dh, (g + 1) * dh) for g in range(heads)]

    @pl.when(pc == 0)
    def _():
        u_ref[...] = _suffix_matrix(u_ref.shape[0])
        qi = lax.broadcasted_iota(jnp.int32, (Ts, Ts), 0)
        ki = lax.broadcasted_iota(jnp.int32, (Ts, Ts), 1)
        outs, cs = _sb_tiles([q_ref[:, c] for c in cols], [kn_ref[:, c] for c in cols],
                             [vn_ref[:, c] for c in cols], [jnp.zeros((Ts, 1), F32)] * heads,
                             _suffix_matrix(Ts), scale, ki < qi)
        for g in range(heads):
            c_ref[g] = cs[g]
            acc_ref[g] = outs[g]

    for g0 in range(0, heads, SB_SAMPLE_BATCH):
        gs = range(g0, g0 + SB_SAMPLE_BATCH)
        outs, cs = _sb_tiles([q_ref[:, cols[g]] for g in gs], [k_all[g] for g in gs],
                             [v_all[g] for g in gs], [c_ref[g] for g in gs], u_ref[...], scale)
        for g, o, c in zip(gs, outs, cs):
            c_ref[g] = c
            acc_ref[g] = acc_ref[g] + o

    @pl.when(pc == pl.num_programs(2) - 1)
    def _():
        for g in range(heads):
            o_ref[:, g * dh:(g + 1) * dh] = acc_ref[g].astype(o_ref.dtype)


def _sb_sample(q, k, v, cache_k, cache_v, layer, row_off, Bs, Ts, H, dh, prev_out):
    P = cache_k.shape[2]
    heads = F32_SUBLANE
    assert H % heads == 0 and row_off % Ts == 0
    cw = _pick(P, SB_CUM, LANE)
    tp = _pick(P, SB_SAMPLE_KEYS, cw)
    npc = P // tp
    ob = row_off // Ts
    new = pl.BlockSpec((Ts, heads * dh), lambda b, g, p: (ob + b, g))
    old = pl.BlockSpec((None, None, tp, heads, dh), lambda b, g, p: (layer, b, npc - 1 - p, g, 0))
    vmem = 2 * 2 * tp * heads * dh * 4 + heads * 16 * Ts * tp * 4 + 8 * MIB
    return pl.pallas_call(
        functools.partial(_sb_sample_kernel, heads=heads, scale=dh ** -0.5),
        out_shape=jax.ShapeDtypeStruct(prev_out.shape, BF16),
        grid=(Bs, H // heads, npc),
        in_specs=[new, new, new, old, old, pl.BlockSpec(memory_space=pl.ANY)],
        out_specs=new,
        scratch_shapes=[pltpu.VMEM((cw, cw), BF16), pltpu.VMEM((heads, Ts, 1), F32),
                        pltpu.VMEM((heads, Ts, dh), F32),
                        pltpu.VMEM((heads, tp, dh), BF16), pltpu.VMEM((heads, tp, dh), BF16)],
        input_output_aliases={5: 0},
        compiler_params=_params(("parallel", "parallel", "arbitrary"), vmem),
        name="sb_sample",
    )(q, k, v, cache_k, cache_v, prev_out)


def kernel(x_prompt, x_sample, cache_pool, cache_k, cache_v, g_mix_pre, g_mix_post, g_ffn_pre, g_ffn_post, pool_w, pool_scale, sgu_w_in, sgu_b_in, sgu_g_v, sgu_w_s, sgu_b_s, sgu_w_out, sb_w_qkv, sb_w_o, ffn_w_up, ffn_w_down):
    B, T, D = x_prompt.shape
    Bs, Ts, _ = x_sample.shape
    Mp, Ms = B * T, Bs * Ts
    M = Mp + Ms
    depth = g_mix_pre.shape[0]
    H, dh = cache_k.shape[3], cache_k.shape[4]
    P = cache_k.shape[2]
    keep = min(T, P)
    W = sgu_g_v.shape[1]

    vec = lambda a, i: a[i].reshape(1, -1)
    pool_w_bf = pool_w.astype(BF16)

    zero_hist = jnp.zeros((B, HALO, D), F32)
    pad_hist = lambda c: jnp.pad(c, ((0, 0), (HALO - POOL_HIST, 0), (0, 0)))

    y = None
    h = None
    pool_hist_p, pool_hist_s, sgu_v_s = [], [], []
    sb_k_p, sb_v_p, sb_k_s, sb_v_s = [], [], [], []
    for i in range(depth):
        kind, j = i % 3, i // 3
        g_ffn = vec(g_ffn_pre, i)
        if kind == 0:
            if y is None:
                src_p, off_p, src_s, off_s = x_prompt.reshape(Mp, D), 0, x_sample.reshape(Ms, D), 0
            else:
                src_p, off_p, src_s, off_s = y, 0, y, Mp
            common = (vec(g_mix_pre, i), pool_w_bf[j], vec(pool_scale, j), vec(g_mix_post, i), g_ffn)
            y_p, h_p, hist_p = _pool_segment(src_p, off_p, B, T, zero_hist, False, M, 0, None, *common)
            y, h, hist_s = _pool_segment(src_s, off_s, Bs, Ts, pad_hist(cache_pool[j]), True, M, Mp,
                                         (y_p, h_p), *common)
            pool_hist_p.append(hist_p[:, HALO - POOL_HIST:])
            pool_hist_s.append(hist_s[:, HALO - POOL_HIST:])
        else:
            if kind == 1:
                b_in = sgu_b_in.reshape(sgu_b_in.shape[0], 1, -1)
                (u,) = _matmul(h, sgu_w_in, j, 0, W, [BF16], act="gelu", bias=b_in)
                (v_raw,) = _matmul(h, sgu_w_in, j, W, W, [F32], act="gelu", bias=b_in)
                gate_args = (vec(sgu_g_v, j), sgu_w_s[j], sgu_b_s[j].T)
                (gated,) = _sgu_gate_segment(u, v_raw, 0, Mp, min(T, SGU_BLOCK), *gate_args, None, False)
                gated, v_s = _sgu_gate_segment(u, v_raw, Mp, Ms, min(Ts, SGU_BLOCK), *gate_args, gated, True)
                sgu_v_s.append(v_s.reshape(Bs, Ts, W))
                (m,) = _matmul(gated, sgu_w_out, j, 0, D, [F32])
            else:
                (q,) = _matmul(h, sb_w_qkv, j, 0, D, [BF16])
                k32, k = _matmul(h, sb_w_qkv, j, D, D, [F32, BF16])
                v32, v = _matmul(h, sb_w_qkv, j, 2 * D, D, [F32, BF16])
                o = _sb_prompt(q, k, v, B, T, H, dh, M)
                o = _sb_sample(q, k, v, cache_k, cache_v, j, Mp, Bs, Ts, H, dh, o)
                (m,) = _matmul(o, sb_w_o, j, 0, D, [F32])
                sb_k_p.append(k32[:Mp].reshape(B, T, H, dh)[:, -keep:])
                sb_v_p.append(v32[:Mp].reshape(B, T, H, dh)[:, -keep:])
                sb_k_s.append(k32[Mp:].reshape(Bs, Ts, H, dh))
                sb_v_s.append(v32[Mp:].reshape(Bs, Ts, H, dh))
            y, h = _resnorm(y, m, vec(g_mix_post, i), g_ffn)
        a, w_down_bf = _matmul(h, ffn_w_up, i, 0, ffn_w_up.shape[2], [BF16], act="relu2",
                               side=(ffn_w_down, i))
        m = _matmul_kgrid(a, w_down_bf, F32)
        g_post = vec(g_ffn_post, i)
        if i + 1 == depth:
            (yp,) = _resnorm(y, m, g_post, None, 0, Mp)
            (ys,) = _resnorm(y, m, g_post, None, Mp, Ms)
        elif (i + 1) % 3 == 0:
            (y,) = _resnorm(y, m, g_post)
        else:
            y, h = _resnorm(y, m, g_post, vec(g_mix_pre, i + 1))
    return (yp.reshape(B, T, D), ys.reshape(Bs, Ts, D),
            jnp.stack(pool_hist_p), jnp.stack(pool_hist_s), jnp.stack(sgu_v_s),
            jnp.stack(sb_k_p), jnp.stack(sb_v_p), jnp.stack(sb_k_s), jnp.stack(sb_v_s))
```

```python
import functools
import math

import jax
import jax.numpy as jnp
from jax import lax
from jax.experimental import pallas as pl
from jax.experimental.pallas import tpu as pltpu

F32 = jnp.float32
BF16 = jnp.bfloat16

NORM_EPS = 1e-6
CHUNK = 64
POOL_WINDOWS = (2, 4, 8, 16)
POOL_HIST = 15
HALO = 16
PAD = 8
SGU_BLOCK = 128
ROW_TILE = 256
MM_ROWS = 1088
MM_COLS = 1024
MM_DEPTH = 4096
SB_QBLOCK = 256
SB_CUM = 256
SB_PROMPT_HEADS = 4
SB_WIDE = 4
SB_SPENT = -110.0
SB_SAMPLE_KEYS = 1024
SB_SAMPLE_BATCH = 8

LANE = 128
F32_SUBLANE = 8
BF16_SUBLANE = 16
VMEM_CAP = 60 * 1024 * 1024
MIB = 1024 * 1024


def _pick(n, cap, mult):
    best = None
    for d in range(mult, min(n, cap) + 1, mult):
        if n % d == 0:
            best = d
    assert best is not None, (n, cap, mult)
    return best


def _params(sem, vmem_bytes):
    return pltpu.CompilerParams(dimension_semantics=sem,
                                vmem_limit_bytes=int(min(VMEM_CAP, vmem_bytes)))


def _rms(x, g):
    ms = jnp.mean(x * x, axis=-1, keepdims=True)
    return x * lax.rsqrt(ms + NORM_EPS) * g


def _activate(acc, act):
    if act == "relu2":
        r = jnp.maximum(acc, 0.0)
        return r * r
    if act == "gelu":
        return 0.5 * acc * (1.0 + lax.erf(acc * (2.0 ** -0.5)))
    return acc


def _mm_kernel(x_ref, w_ref, o_ref, acc_ref, *, nk):
    k = pl.program_id(2)
    dot = lambda: jnp.dot(x_ref[...], w_ref[...], preferred_element_type=F32)
    if nk == 1:
        o_ref[...] = dot().astype(o_ref.dtype)
        return

    @pl.when(k == 0)
    def _():
        acc_ref[...] = dot()

    @pl.when(jnp.logical_and(k > 0, k < nk - 1))
    def _():
        acc_ref[...] += dot()

    @pl.when(k == nk - 1)
    def _():
        o_ref[...] = (acc_ref[...] + dot()).astype(o_ref.dtype)


def _matmul_kgrid(x, w, out_dtype):
    M, K = x.shape
    N = w.shape[1]
    tm = _pick(M, MM_ROWS, BF16_SUBLANE)
    tn = _pick(N, MM_COLS, LANE)
    tk = _pick(K, MM_DEPTH, LANE)
    nk = K // tk
    vmem = (2 * tm * tk * 2 + 2 * tk * tn * 2 + 2 * tm * tn * jnp.dtype(out_dtype).itemsize
            + 3 * tm * tn * 4 + 4 * MIB)
    return pl.pallas_call(
        functools.partial(_mm_kernel, nk=nk),
        out_shape=jax.ShapeDtypeStruct((M, N), out_dtype),
        grid=(N // tn, M // tm, nk),
        in_specs=[pl.BlockSpec((tm, tk), lambda n, m, k: (m, k)),
                  pl.BlockSpec((tk, tn), lambda n, m, k: (k, n))],
        out_specs=pl.BlockSpec((tm, tn), lambda n, m, k: (m, n)),
        scratch_shapes=[pltpu.VMEM((tm, tn), F32)],
        compiler_params=_params(("parallel", "parallel", "arbitrary"), vmem),
        name="matmul_kgrid",
    )(x, w)


def _mm_ws_kernel(*refs, act, has_bias, n_out, has_side, rows):
    x_ref, w_ref = refs[0], refs[1]
    pos = 2
    b_ref = refs[pos] if has_bias else None
    pos += has_bias
    side_in = refs[pos] if has_side else None
    pos += has_side
    outs = refs[pos:pos + n_out]
    pos += n_out
    side_out = refs[pos] if has_side else None
    pos += has_side
    wbf_ref = refs[pos]
    n, m = pl.program_id(0), pl.program_id(1)
    last = pl.num_programs(0) - 1

    def multiply():
        acc = jnp.dot(x_ref[...], wbf_ref[(n - 1) % 2], preferred_element_type=F32)
        if has_bias:
            acc = acc + b_ref[...]
        acc = _activate(acc, act)
        for o in outs:
            o[...] = acc.astype(o.dtype)
        if has_side:
            side_out[...] = side_in[...].astype(side_out.dtype)

    def cast_next():
        r0 = pl.multiple_of(m * rows, rows)
        wbf_ref[n % 2, pl.ds(r0, rows), :] = w_ref[...].astype(BF16)

    @pl.when(n == 0)
    def _():
        cast_next()

    @pl.when(jnp.logical_and(n > 0, n < last))
    def _():
        multiply()
        cast_next()

    @pl.when(n == last)
    def _():
        multiply()


def _matmul(x, w, layer, n_off, n_cols, out_dtypes, act="none", bias=None, side=None):
    M, K = x.shape
    tm = _pick(M, MM_ROWS, BF16_SUBLANE)
    tn = _pick(n_cols, MM_COLS, LANE)
    mt, nt = M // tm, n_cols // tn
    rows = K // mt
    assert K <= MM_DEPTH and K % mt == 0 and rows % BF16_SUBLANE == 0 and n_off % tn == 0
    nb_off = n_off // tn
    row_of = lambda n, m: jnp.where(n > 0, m, 0)
    col_of = lambda n: jnp.maximum(n - 1, 0)
    in_specs = [
        pl.BlockSpec((tm, K), lambda n, m: (row_of(n, m), 0)),
        pl.BlockSpec((None, rows, tn),
                     lambda n, m: (layer, jnp.where(n < nt, m, mt - 1), jnp.minimum(n, nt - 1) + nb_off)),
    ]
    args = [x, w]
    if bias is not None:
        in_specs.append(pl.BlockSpec((None, 1, tn), lambda n, m: (layer, 0, col_of(n) + nb_off)))
        args.append(bias)
    out_shape = [jax.ShapeDtypeStruct((M, n_cols), d) for d in out_dtypes]
    out_specs = [pl.BlockSpec((tm, tn), lambda n, m: (row_of(n, m), col_of(n))) for _ in out_dtypes]
    out_bytes = sum(jnp.dtype(d).itemsize for d in out_dtypes)
    vmem = (2 * tm * K * 2 + 2 * K * tn * 2 + 2 * rows * tn * 4 + 2 * tm * tn * out_bytes
            + 2 * tm * tn * 4 + 4 * MIB)
    if side is not None:
        s_arr, s_idx = side
        R, C = s_arr.shape[1:]
        rs = R // (nt * mt)
        assert R % (nt * mt) == 0 and rs % BF16_SUBLANE == 0
        slab = lambda n, m: jnp.maximum((n - 1) * mt + m, 0)
        in_specs.append(pl.BlockSpec((None, rs, C), lambda n, m: (s_idx, slab(n, m), 0)))
        args.append(s_arr)
        out_shape.append(jax.ShapeDtypeStruct((R, C), BF16))
        out_specs.append(pl.BlockSpec((rs, C), lambda n, m: (slab(n, m), 0)))
        vmem += 2 * rs * C * (4 + 2)
    return pl.pallas_call(
        functools.partial(_mm_ws_kernel, act=act, has_bias=bias is not None, n_out=len(out_dtypes),
                          has_side=side is not None, rows=rows),
        out_shape=out_shape,
        grid=(nt + 1, mt),
        in_specs=in_specs,
        out_specs=out_specs,
        scratch_shapes=[pltpu.VMEM((2, K, tn), BF16)],
        compiler_params=_params(("arbitrary", "arbitrary"), vmem),
        name="matmul_" + act,
    )(*args)


def _resnorm_kernel(*refs, has_next):
    y_ref, m_ref, gpost_ref = refs[0], refs[1], refs[2]
    gnext_ref = refs[3] if has_next else None
    ynew_ref = refs[3 + has_next]
    ynew = y_ref[...] + _rms(m_ref[...], gpost_ref[...])
    ynew_ref[...] = ynew
    if has_next:
        refs[5][...] = _rms(ynew, gnext_ref[...]).astype(BF16)


def _resnorm(y, m, g_post, g_next=None, row_off=0, rows=None):
    D = y.shape[1]
    rows = y.shape[0] if rows is None else rows
    tm = _pick(rows, ROW_TILE, BF16_SUBLANE)
    assert row_off % tm == 0
    ob = row_off // tm
    has_next = g_next is not None
    row = pl.BlockSpec((tm, D), lambda i: (i + ob, 0))
    out_row = pl.BlockSpec((tm, D), lambda i: (i, 0))
    vec = pl.BlockSpec((1, D), lambda i: (0, 0))
    out_shape = [jax.ShapeDtypeStruct((rows, D), F32)]
    if has_next:
        out_shape.append(jax.ShapeDtypeStruct((rows, D), BF16))
    vmem = 2 * tm * D * (4 + 4 + 4 + 2) + 4 * tm * D * 4 + 4 * MIB
    return pl.pallas_call(
        functools.partial(_resnorm_kernel, has_next=has_next),
        out_shape=out_shape,
        grid=(rows // tm,),
        in_specs=[row, row, vec] + ([vec] if has_next else []),
        out_specs=[out_row] * len(out_shape),
        compiler_params=_params(("parallel",), vmem),
        name="resnorm",
    )(*([y, m, g_post] + ([g_next] if has_next else [])))


def _pool_kernel(y_ref, hist_ref, gpre_ref, w_ref, scale_ref, gpost_ref, gnext_ref,
                 ynew_ref, hnext_ref, histout_ref, buf_ref, m_ref, lvl_a, lvl_b, *, nseq, tm, hist_valid):
    @pl.when(pl.program_id(0) < nseq)
    def _():
        _pool_tile(y_ref, hist_ref, gpre_ref, w_ref, scale_ref, gpost_ref, gnext_ref,
                   ynew_ref, hnext_ref, histout_ref, buf_ref, m_ref, lvl_a, lvl_b,
                   tm=tm, hist_valid=hist_valid)

    @pl.when(pl.program_id(0) >= nseq)
    def _():
        ynew_ref[...] = jnp.zeros(ynew_ref.shape, ynew_ref.dtype)
        hnext_ref[...] = jnp.zeros(hnext_ref.shape, hnext_ref.dtype)


def _pool_tile(y_ref, hist_ref, gpre_ref, w_ref, scale_ref, gpost_ref, gnext_ref,
               ynew_ref, hnext_ref, histout_ref, buf_ref, m_ref, lvl_a, lvl_b, *, tm, hist_valid):
    t = pl.program_id(1)
    D = y_ref.shape[-1]
    gw = D // len(POOL_WINDOWS)
    BASE = PAD + HALO
    n = HALO + tm
    for ref in (buf_ref, lvl_a, lvl_b):
        ref[0:PAD, :] = jnp.zeros((PAD, ref.shape[1]), F32)

    @pl.when(t == 0)
    def _():
        buf_ref[PAD:BASE, :] = hist_ref[...]

    @pl.when(t > 0)
    def _():
        buf_ref[PAD:BASE, :] = buf_ref[PAD + tm:BASE + tm, :]

    y = y_ref[...]
    h = _rms(y, gpre_ref[...])
    buf_ref[BASE:BASE + tm, :] = h
    histout_ref[...] = h[tm - HALO:, :]

    def window_sum(cols, w):
        read = lambda lo, rows: buf_ref[lo:lo + rows, cols]
        shift, dst, other = 1, lvl_a, lvl_b
        while 2 * shift < w:
            dst[PAD:PAD + n, :] = read(PAD, n) + read(PAD - shift, n)
            read = lambda lo, rows, src=dst: src[lo:lo + rows, :]
            shift, dst, other = 2 * shift, other, dst
        return read(BASE, tm) + read(BASE - shift, tm)

    pos = t * tm + lax.broadcasted_iota(jnp.int32, (tm, 1), 0)
    ssq = jnp.zeros((tm, 1), F32)
    for g, w in enumerate(POOL_WINDOWS):
        cols = slice(g * gw, (g + 1) * gw)
        hg = buf_ref[BASE:BASE + tm, cols]
        s = window_sum(cols, w)
        if hist_valid:
            mean = s / float(w)
        else:
            mean = s / jnp.minimum(pos + 1, w).astype(F32)
        d = mean - hg
        mg = jnp.dot(d.astype(BF16), w_ref[g], preferred_element_type=F32) * scale_ref[:, cols]
        m_ref[:, cols] = mg
        ssq = ssq + jnp.sum(mg * mg, axis=-1, keepdims=True)
    ynew = y + m_ref[...] * lax.rsqrt(ssq / D + NORM_EPS) * gpost_ref[...]
    ynew_ref[...] = ynew
    hnext_ref[...] = _rms(ynew, gnext_ref[...]).astype(BF16)


def _pool_segment(y, row_off, nseq, T, hist, hist_valid, out_rows, out_off, prev_outs,
                  g_pre, w_bf, scale, g_post, g_next):
    D = y.shape[1]
    G = len(POOL_WINDOWS)
    gw = D // G
    rest = 0 if prev_outs is not None else out_rows - out_off - nseq * T
    tm = _pick(math.gcd(T, rest), ROW_TILE, BF16_SUBLANE)
    nt = T // tm
    assert row_off % tm == 0 and out_off % tm == 0
    ib, ob = row_off // tm, out_off // tm
    fill = rest // tm
    assert 0 <= fill <= nt
    tile_of = lambda s, t: jnp.minimum(s * nt + t, nseq * nt - 1)
    seq_of = lambda s: jnp.minimum(s, nseq - 1)
    vec = pl.BlockSpec((1, D), lambda s, t: (0, 0))
    in_specs = [
        pl.BlockSpec((tm, D), lambda s, t: (ib + tile_of(s, t), 0)),
        pl.BlockSpec((None, HALO, D), lambda s, t: (seq_of(s), 0, 0)),
        vec,
        pl.BlockSpec((G, gw, gw), lambda s, t: (0, 0, 0)),
        vec, vec, vec,
    ]
    args = [y, hist, g_pre, w_bf, scale, g_post, g_next]
    aliases = {}
    if prev_outs is not None:
        in_specs += [pl.BlockSpec(memory_space=pl.ANY)] * 2
        args += list(prev_outs)
        aliases = {7: 0, 8: 1}
    out_row = pl.BlockSpec((tm, D), lambda s, t: (ob + jnp.minimum(s * nt + t, nseq * nt + fill - 1), 0))
    vmem = (2 * tm * D * (4 + 4 + 2) + 2 * G * gw * gw * 2 + 2 * (tm + HALO) * D * 4
            + 3 * tm * D * 4 + 4 * MIB)
    kern = functools.partial(_pool_kernel, nseq=nseq, tm=tm, hist_valid=hist_valid)
    if prev_outs is not None:
        def kern(*refs, _k=kern):
            return _k(*refs[:7], *refs[9:])
    return pl.pallas_call(
        kern,
        out_shape=[jax.ShapeDtypeStruct((out_rows, D), F32),
                   jax.ShapeDtypeStruct((out_rows, D), BF16),
                   jax.ShapeDtypeStruct((nseq, HALO, D), F32)],
        grid=(nseq + (fill > 0), nt),
        in_specs=in_specs,
        out_specs=[out_row, out_row, pl.BlockSpec((None, HALO, D), lambda s, t: (seq_of(s), 0, 0))],
        scratch_shapes=[pltpu.VMEM((PAD + HALO + tm, D), F32), pltpu.VMEM((tm, D), F32),
                        pltpu.VMEM((PAD + HALO + tm, gw), F32), pltpu.VMEM((PAD + HALO + tm, gw), F32)],
        input_output_aliases=aliases,
        compiler_params=_params(("parallel", "arbitrary"), vmem),
        name="pool_mixer",
    )(*args)


def _sgu_gate_kernel(*refs, real, blk, nblk, groups, has_prev, emit_v):
    outs = refs[5 + has_prev:]

    @pl.when(pl.program_id(0) < real)
    def _():
        _sgu_gate_tile(*refs[:5], *outs, blk=blk, nblk=nblk, groups=groups, emit_v=emit_v)

    @pl.when(pl.program_id(0) >= real)
    def _():
        outs[0][...] = jnp.zeros(outs[0].shape, outs[0].dtype)


def _sgu_gate_tile(u_ref, v_ref, gv_ref, ws_ref, bst_ref, out_ref, *rest, blk, nblk, groups, emit_v):
    vout_ref = rest[0] if emit_v else None
    W = u_ref.shape[-1]
    gw = W // groups
    ri = lax.broadcasted_iota(jnp.int32, (SGU_BLOCK, SGU_BLOCK), 0)
    ci = lax.broadcasted_iota(jnp.int32, (SGU_BLOCK, SGU_BLOCK), 1)
    mask = (ci // CHUNK) <= (ri // CHUNK)
    wm = [jnp.where(mask, ws_ref[g], 0.0).astype(BF16)[:blk, :] for g in range(groups)]
    bias = [bst_ref[:blk, g:g + 1] for g in range(groups)]
    for b in range(nblk):
        rows = slice(b * blk, (b + 1) * blk)
        v = _rms(v_ref[rows, :], gv_ref[...])
        if emit_v:
            vout_ref[rows, :] = v
        vb = v.astype(BF16)
        if blk < SGU_BLOCK:
            vb = jnp.concatenate([vb, jnp.zeros((SGU_BLOCK - blk, W), BF16)], axis=0)
        for g in range(groups):
            cols = slice(g * gw, (g + 1) * gw)
            f = jnp.dot(wm[g], vb[:, cols], preferred_element_type=F32) + bias[g]
            out_ref[rows, cols] = (u_ref[rows, cols].astype(F32) * f).astype(BF16)


def _sgu_gate_segment(u, v_raw, row_off, rows, blk, g_v, w_s, b_s_t, prev_out, emit_v):
    M, W = u.shape
    groups = w_s.shape[0]
    rest = 0 if prev_out is not None else M - row_off - rows
    tm = _pick(math.gcd(rows, rest), ROW_TILE, blk)
    nblk = tm // blk
    assert row_off % tm == 0
    ob = row_off // tm
    real, fill = rows // tm, rest // tm
    row = lambda i: (i + ob, 0)
    src = lambda i: (jnp.minimum(i, real - 1) + ob, 0)
    in_specs = [
        pl.BlockSpec((tm, W), src),
        pl.BlockSpec((tm, W), src),
        pl.BlockSpec((1, W), lambda i: (0, 0)),
        pl.BlockSpec((groups, SGU_BLOCK, SGU_BLOCK), lambda i: (0, 0, 0)),
        pl.BlockSpec((SGU_BLOCK, groups), lambda i: (0, 0)),
    ]
    args = [u, v_raw, g_v, w_s, b_s_t]
    aliases = {}
    if prev_out is not None:
        in_specs.append(pl.BlockSpec(memory_space=pl.ANY))
        args.append(prev_out)
        aliases = {5: 0}
    out_shape = [jax.ShapeDtypeStruct((M, W), BF16)]
    out_specs = [pl.BlockSpec((tm, W), row)]
    if emit_v:
        out_shape.append(jax.ShapeDtypeStruct((rows, W), F32))
        out_specs.append(pl.BlockSpec((tm, W), lambda i: (i, 0)))
    vmem = 2 * tm * W * (2 + 4 + 2 + 4) + 4 * tm * W * 4 + 4 * MIB
    return pl.pallas_call(
        functools.partial(_sgu_gate_kernel, real=real, blk=blk, nblk=nblk, groups=groups,
                          has_prev=prev_out is not None, emit_v=emit_v),
        out_shape=out_shape,
        grid=(real + fill,),
        in_specs=in_specs,
        out_specs=out_specs,
        input_output_aliases=aliases,
        compiler_params=_params(("parallel",), vmem),
        name="sgu_gate",
    )(*args)


def _sb_logs(z):
    ls = jnp.minimum(z, 0.0) - jnp.log(1.0 + jnp.exp(-jnp.abs(z)))
    return ls, ls - z


def _suffix_matrix(n):
    j = lax.broadcasted_iota(jnp.int32, (n, n), 0)
    s = lax.broadcasted_iota(jnp.int32, (n, n), 1)
    return jnp.where(j > s, 1.0, 0.0).astype(BF16)


def _suffix_sum(ln, U):
    n = ln.shape[0]
    hi = ln.astype(BF16)
    lo = (ln - hi.astype(F32)).astype(BF16)
    r = jnp.dot(jnp.concatenate([hi, lo], axis=0), U, preferred_element_type=F32)
    return r[:n] + r[n:]


def _sb_live(c_ref):
    c = c_ref[0]
    for g in range(1, c_ref.shape[0]):
        c = jnp.maximum(c, c_ref[g])
    return (jnp.max(c) > SB_SPENT).astype(jnp.int32)


def _sb_tiles(qs, ks, vs, cs, U, scale, allowed=None):
    dn = (((1,), (1,)), ((), ()))
    zs = [lax.dot_general(q, k, dn, preferred_element_type=F32) * scale for q, k in zip(qs, ks)]
    cw = U.shape[0]
    chunks = [slice(j * cw, (j + 1) * cw) for j in range(zs[0].shape[1] // cw)]
    logs = []
    for z in zs:
        ls, ln = _sb_logs(z)
        if allowed is not None:
            ln = jnp.where(allowed, ln, 0.0)
        logs.append((ls, [_suffix_sum(ln[:, cols], U) for cols in chunks],
                     [jnp.sum(ln[:, cols], axis=1, keepdims=True) for cols in chunks]))
    outs, new_cs = [], []
    for (ls, suffix, total), v, c in zip(logs, vs, cs):
        parts = [None] * len(chunks)
        for j in reversed(range(len(chunks))):
            parts[j] = jnp.exp(ls[:, chunks[j]] + suffix[j] + c)
            c = c + total[j]
        a = parts[0] if len(parts) == 1 else jnp.concatenate(parts, axis=1)
        if allowed is not None:
            a = jnp.where(allowed, a, 0.0)
        outs.append(jnp.dot(a.astype(BF16), v, preferred_element_type=F32))
        new_cs.append(c)
    return outs, new_cs


def _sb_prompt_kernel(q_ref, k_ref, v_ref, o_ref, u_ref, c_ref, acc_ref,
                      *, nseq, tq, wide, heads, dh, scale):
    T = q_ref.shape[0]
    cols = [slice(g * dh, (g + 1) * dh) for g in range(heads)]

    def tiles(qs, ks, width, diagonal):
        allowed = None
        if diagonal:
            qi = lax.broadcasted_iota(jnp.int32, (tq, tq), 0)
            ki = lax.broadcasted_iota(jnp.int32, (tq, tq), 1)
            allowed = ki < qi
        outs, cs = _sb_tiles([q_ref[pl.ds(qs, tq), c] for c in cols],
                             [k_ref[pl.ds(ks, width * tq), c] for c in cols],
                             [v_ref[pl.ds(ks, width * tq), c] for c in cols],
                             [jnp.zeros((tq, 1), F32) if diagonal else c_ref[g] for g in range(heads)],
                             u_ref[...], scale, allowed)
        for g in range(heads):
            c_ref[g] = cs[g]
            acc_ref[g] = outs[g] if diagonal else acc_ref[g] + outs[g]

    def q_body(i, carry):
        qs = pl.multiple_of(i * tq, tq)
        tiles(qs, qs, 1, True)
        lead = jnp.where(i > 0, (i - 1) % wide + 1, 0)

        def single_body(st):
            tiles(qs, pl.multiple_of((i - 1 - st[0]) * tq, tq), 1, False)
            return st[0] + 1, _sb_live(c_ref)

        _, live = lax.while_loop(lambda st: jnp.logical_and(st[0] < lead, st[1] > 0), single_body,
                                 (jnp.int32(0), _sb_live(c_ref)))

        def wide_body(st):
            tiles(qs, pl.multiple_of((i - lead - wide * (st[0] + 1)) * tq, tq), wide, False)
            return st[0] + 1, _sb_live(c_ref)

        lax.while_loop(lambda st: jnp.logical_and(st[0] < (i - lead) // wide, st[1] > 0), wide_body,
                       (jnp.int32(0), live))
        for g in range(heads):
            o_ref[pl.ds(qs, tq), cols[g]] = acc_ref[g].astype(o_ref.dtype)
        return carry

    @pl.when(pl.program_id(0) < nseq)
    def _():
        u_ref[...] = _suffix_matrix(tq)
        lax.fori_loop(0, T // tq, q_body, 0)

    @pl.when(pl.program_id(0) >= nseq)
    def _():
        o_ref[...] = jnp.zeros(o_ref.shape, o_ref.dtype)


def _sb_prompt(q, k, v, B, T, H, dh, out_rows):
    tq = _pick(T, SB_QBLOCK, LANE)
    heads = _pick(H, SB_PROMPT_HEADS, 1)
    nb = pl.cdiv(out_rows, T)
    src = pl.BlockSpec((T, heads * dh),
                       lambda b, h: (jnp.minimum(b, B - 1), jnp.where(b < B, h, H // heads - 1)))
    vmem = 2 * 4 * T * heads * dh * 2 + heads * SB_WIDE * 24 * tq * tq * 4 + 4 * MIB
    return pl.pallas_call(
        functools.partial(_sb_prompt_kernel, nseq=B, tq=tq, wide=SB_WIDE, heads=heads, dh=dh,
                          scale=dh ** -0.5),
        out_shape=jax.ShapeDtypeStruct((out_rows, H * dh), BF16),
        grid=(nb, H // heads),
        in_specs=[src, src, src],
        out_specs=pl.BlockSpec((T, heads * dh), lambda b, h: (b, h)),
        scratch_shapes=[pltpu.VMEM((tq, tq), BF16), pltpu.VMEM((heads, tq, 1), F32),
                        pltpu.VMEM((heads, tq, dh), F32)],
        compiler_params=_params(("parallel", "parallel"), vmem),
        name="sb_prompt",
    )(q, k, v)


def _sb_sample_kernel(q_ref, kn_ref, vn_ref, kc_ref, vc_ref, prev_ref, o_ref, u_ref, c_ref, acc_ref,
                      k_all, v_all, *, heads, scale):
    del prev_ref
    pc = pl.program_id(2)
    Ts = q_ref.shape[0]
    dh = kc_ref.shape[2]
    cols = [slice(g * dh, (g + 1) * dh) for g in range(heads)]

    @pl.when(pc == 0)
    def _():
        u_ref[...] = _suffix_matrix(u_ref.shape[0])
        qi = lax.broadcasted_iota(jnp.int32, (Ts, Ts), 0)
        ki = lax.broadcasted_iota(jnp.int32, (Ts, Ts), 1)
        outs, cs = _sb_tiles([q_ref[:, c] for c in cols], [kn_ref[:, c] for c in cols],
                             [vn_ref[:, c] for c in cols], [jnp.zeros((Ts, 1), F32)] * heads,
                             _suffix_matrix(Ts), scale, ki < qi)
        for g in range(heads):
            c_ref[g] = cs[g]
            acc_ref[g] = outs[g]

    @pl.when(_sb_live(c_ref) > 0)
    def _():
        k_all[...] = pltpu.einshape("phd->hpd", kc_ref[...].astype(BF16))
        v_all[...] = pltpu.einshape("phd->hpd", vc_ref[...].astype(BF16))
        for g0 in range(0, heads, SB_SAMPLE_BATCH):
            gs = range(g0, g0 + SB_SAMPLE_BATCH)
            outs, cs = _sb_tiles([q_ref[:, cols[g]] for g in gs], [k_all[g] for g in gs],
                                 [v_all[g] for g in gs], [c_ref[g] for g in gs], u_ref[...], scale)
            for g, o, c in zip(gs, outs, cs):
                c_ref[g] = c
                acc_ref[g] = acc_ref[g] + o

    @pl.when(pc == pl.num_programs(2) - 1)
    def _():
        for g in range(heads):
            o_ref[:, g * dh:(g + 1) * dh] = acc_ref[g].astype(o_ref.dtype)


def _sb_sample(q, k, v, cache_k, cache_v, layer, row_off, Bs, Ts, H, dh, prev_out):
    P = cache_k.shape[2]
    heads = F32_SUBLANE
    assert H % heads == 0 and row_off % Ts == 0
    cw = _pick(P, SB_CUM, LANE)
    tp = _pick(P, SB_SAMPLE_KEYS, cw)
    npc = P // tp
    ob = row_off // Ts
    new = pl.BlockSpec((Ts, heads * dh), lambda b, g, p: (ob + b, g))
    old = pl.BlockSpec((None, None, tp, heads, dh), lambda b, g, p: (layer, b, npc - 1 - p, g, 0))
    vmem = 2 * 2 * tp * heads * dh * 4 + heads * 16 * Ts * tp * 4 + 8 * MIB
    return pl.pallas_call(
        functools.partial(_sb_sample_kernel, heads=heads, scale=dh ** -0.5),
        out_shape=jax.ShapeDtypeStruct(prev_out.shape, BF16),
        grid=(Bs, H // heads, npc),
        in_specs=[new, new, new, old, old, pl.BlockSpec(memory_space=pl.ANY)],
        out_specs=new,
        scratch_shapes=[pltpu.VMEM((cw, cw), BF16), pltpu.VMEM((heads, Ts, 1), F32),
                        pltpu.VMEM((heads, Ts, dh), F32),
                        pltpu.VMEM((heads, tp, dh), BF16), pltpu.VMEM((heads, tp, dh), BF16)],
        input_output_aliases={5: 0},
        compiler_params=_params(("parallel", "parallel", "arbitrary"), vmem),
        name="sb_sample",
    )(q, k, v, cache_k, cache_v, prev_out)


def kernel(x_prompt, x_sample, cache_pool, cache_k, cache_v, g_mix_pre, g_mix_post, g_ffn_pre, g_ffn_post, pool_w, pool_scale, sgu_w_in, sgu_b_in, sgu_g_v, sgu_w_s, sgu_b_s, sgu_w_out, sb_w_qkv, sb_w_o, ffn_w_up, ffn_w_down):
    B, T, D = x_prompt.shape
    Bs, Ts, _ = x_sample.shape
    Mp, Ms = B * T, Bs * Ts
    M = Mp + Ms
    depth = g_mix_pre.shape[0]
    H, dh = cache_k.shape[3], cache_k.shape[4]
    P = cache_k.shape[2]
    keep = min(T, P)
    W = sgu_g_v.shape[1]

    vec = lambda a, i: a[i].reshape(1, -1)
    pool_w_bf = pool_w.astype(BF16)

    zero_hist = jnp.zeros((B, HALO, D), F32)
    pad_hist = lambda c: jnp.pad(c, ((0, 0), (HALO - POOL_HIST, 0), (0, 0)))

    y = None
    h = None
    pool_hist_p, pool_hist_s, sgu_v_s = [], [], []
    sb_k_p, sb_v_p, sb_k_s, sb_v_s = [], [], [], []
    for i in range(depth):
        kind, j = i % 3, i // 3
        g_ffn = vec(g_ffn_pre, i)
        if kind == 0:
            if y is None:
                src_p, off_p, src_s, off_s = x_prompt.reshape(Mp, D), 0, x_sample.reshape(Ms, D), 0
            else:
                src_p, off_p, src_s, off_s = y, 0, y, Mp
            common = (vec(g_mix_pre, i), pool_w_bf[j], vec(pool_scale, j), vec(g_mix_post, i), g_ffn)
            y_p, h_p, hist_p = _pool_segment(src_p, off_p, B, T, zero_hist, False, M, 0, None, *common)
            y, h, hist_s = _pool_segment(src_s, off_s, Bs, Ts, pad_hist(cache_pool[j]), True, M, Mp,
                                         (y_p, h_p), *common)
            pool_hist_p.append(hist_p[:, HALO - POOL_HIST:])
            pool_hist_s.append(hist_s[:, HALO - POOL_HIST:])
        else:
            if kind == 1:
                b_in = sgu_b_in.reshape(sgu_b_in.shape[0], 1, -1)
                (u,) = _matmul(h, sgu_w_in, j, 0, W, [BF16], act="gelu", bias=b_in)
                (v_raw,) = _matmul(h, sgu_w_in, j, W, W, [F32], act="gelu", bias=b_in)
                gate_args = (vec(sgu_g_v, j), sgu_w_s[j], sgu_b_s[j].T)
                (gated,) = _sgu_gate_segment(u, v_raw, 0, Mp, min(T, SGU_BLOCK), *gate_args, None, False)
                gated, v_s = _sgu_gate_segment(u, v_raw, Mp, Ms, min(Ts, SGU_BLOCK), *gate_args, gated, True)
                sgu_v_s.append(v_s.reshape(Bs, Ts, W))
                (m,) = _matmul(gated, sgu_w_out, j, 0, D, [F32])
            else:
                (q,) = _matmul(h, sb_w_qkv, j, 0, D, [BF16])
                k32, k = _matmul(h, sb_w_qkv, j, D, D, [F32, BF16])
                v32, v = _matmul(h, sb_w_qkv, j, 2 * D, D, [F32, BF16])
                o = _sb_prompt(q, k, v, B, T, H, dh, M)
                o = _sb_sample(q, k, v, cache_k, cache_v, j, Mp, Bs, Ts, H, dh, o)
                (m,) = _matmul(o, sb_w_o, j, 0, D, [F32])
                sb_k_p.append(k32[:Mp].reshape(B, T, H, dh)[:, -keep:])
                sb_v_p.append(v32[:Mp].reshape(B, T, H, dh)[:, -keep:])
                sb_k_s.append(k32[Mp:].reshape(Bs, Ts, H, dh))
                sb_v_s.append(v32[Mp:].reshape(Bs, Ts, H, dh))
            y, h = _resnorm(y, m, vec(g_mix_post, i), g_ffn)
        a, w_down_bf = _matmul(h, ffn_w_up, i, 0, ffn_w_up.shape[2], [BF16], act="relu2",
                               side=(ffn_w_down, i))
        m = _matmul_kgrid(a, w_down_bf, F32)
        g_post = vec(g_ffn_post, i)
        if i + 1 == depth:
            (yp,) = _resnorm(y, m, g_post, None, 0, Mp)
            (ys,) = _resnorm(y, m, g_post, None, Mp, Ms)
        elif (i + 1) % 3 == 0:
            (y,) = _resnorm(y, m, g_post)
        else:
            y, h = _resnorm(y, m, g_post, vec(g_mix_pre, i + 1))
    return (yp.reshape(B, T, D), ys.reshape(Bs, Ts, D),
            jnp.stack(pool_hist_p), jnp.stack(pool_hist_s), jnp.stack(sgu_v_s),
            jnp.stack(sb_k_p), jnp.stack(sb_v_p), jnp.stack(sb_k_s), jnp.stack(sb_v_s))
```

```python
import functools
import math

import jax
import jax.numpy as jnp
from jax import lax
from jax.experimental import pallas as pl
from jax.experimental.pallas import tpu as pltpu

F32 = jnp.float32
BF16 = jnp.bfloat16

NORM_EPS = 1e-6
CHUNK = 64
POOL_WINDOWS = (2, 4, 8, 16)
POOL_HIST = 15
HALO = 16
PAD = 8
SGU_BLOCK = 128
ROW_TILE = 256
MM_ROWS = 1088
MM_COLS = 1024
MM_DEPTH = 4096
SB_QBLOCK = 256
SB_CUM = 256
SB_PROMPT_HEADS = 4
SB_WIDE = 4
SB_SPENT = -110.0
SB_SAMPLE_KEYS = 1024
SB_SAMPLE_BATCH = 8

LANE = 128
F32_SUBLANE = 8
BF16_SUBLANE = 16
VMEM_CAP = 60 * 1024 * 1024
MIB = 1024 * 1024


def _pick(n, cap, mult):
    best = None
    for d in range(mult, min(n, cap) + 1, mult):
        if n % d == 0:
            best = d
    assert best is not None, (n, cap, mult)
    return best


def _params(sem, vmem_bytes):
    return pltpu.CompilerParams(dimension_semantics=sem,
                                vmem_limit_bytes=int(min(VMEM_CAP, vmem_bytes)))


def _rms(x, g):
    ms = jnp.mean(x * x, axis=-1, keepdims=True)
    return x * lax.rsqrt(ms + NORM_EPS) * g


def _activate(acc, act):
    if act == "relu2":
        r = jnp.maximum(acc, 0.0)
        return r * r
    if act == "gelu":
        return 0.5 * acc * (1.0 + lax.erf(acc * (2.0 ** -0.5)))
    return acc


def _mm_kernel(x_ref, w_ref, o_ref, acc_ref, *, nk):
    k = pl.program_id(2)
    dot = lambda: jnp.dot(x_ref[...], w_ref[...], preferred_element_type=F32)
    if nk == 1:
        o_ref[...] = dot().astype(o_ref.dtype)
        return

    @pl.when(k == 0)
    def _():
        acc_ref[...] = dot()

    @pl.when(jnp.logical_and(k > 0, k < nk - 1))
    def _():
        acc_ref[...] += dot()

    @pl.when(k == nk - 1)
    def _():
        o_ref[...] = (acc_ref[...] + dot()).astype(o_ref.dtype)


def _matmul_kgrid(x, w, out_dtype):
    M, K = x.shape
    N = w.shape[1]
    tm = _pick(M, MM_ROWS, BF16_SUBLANE)
    tn = _pick(N, MM_COLS, LANE)
    tk = _pick(K, MM_DEPTH, LANE)
    nk = K // tk
    vmem = (2 * tm * tk * 2 + 2 * tk * tn * 2 + 2 * tm * tn * jnp.dtype(out_dtype).itemsize
            + 3 * tm * tn * 4 + 4 * MIB)
    return pl.pallas_call(
        functools.partial(_mm_kernel, nk=nk),
        out_shape=jax.ShapeDtypeStruct((M, N), out_dtype),
        grid=(N // tn, M // tm, nk),
        in_specs=[pl.BlockSpec((tm, tk), lambda n, m, k: (m, k)),
                  pl.BlockSpec((tk, tn), lambda n, m, k: (k, n))],
        out_specs=pl.BlockSpec((tm, tn), lambda n, m, k: (m, n)),
        scratch_shapes=[pltpu.VMEM((tm, tn), F32)],
        compiler_params=_params(("parallel", "parallel", "arbitrary"), vmem),
        name="matmul_kgrid",
    )(x, w)


def _mm_ws_kernel(*refs, act, has_bias, n_out, has_side, rows):
    x_ref, w_ref = refs[0], refs[1]
    pos = 2
    b_ref = refs[pos] if has_bias else None
    pos += has_bias
    side_in = refs[pos] if has_side else None
    pos += has_side
    outs = refs[pos:pos + n_out]
    pos += n_out
    side_out = refs[pos] if has_side else None
    pos += has_side
    wbf_ref = refs[pos]
    n, m = pl.program_id(0), pl.program_id(1)
    last = pl.num_programs(0) - 1

    def multiply():
        acc = jnp.dot(x_ref[...], wbf_ref[(n - 1) % 2], preferred_element_type=F32)
        if has_bias:
            acc = acc + b_ref[...]
        acc = _activate(acc, act)
        for o in outs:
            o[...] = acc.astype(o.dtype)
        if has_side:
            side_out[...] = side_in[...].astype(side_out.dtype)

    def cast_next():
        r0 = pl.multiple_of(m * rows, rows)
        wbf_ref[n % 2, pl.ds(r0, rows), :] = w_ref[...].astype(BF16)

    @pl.when(n == 0)
    def _():
        cast_next()

    @pl.when(jnp.logical_and(n > 0, n < last))
    def _():
        multiply()
        cast_next()

    @pl.when(n == last)
    def _():
        multiply()


def _matmul(x, w, layer, n_off, n_cols, out_dtypes, act="none", bias=None, side=None):
    M, K = x.shape
    tm = _pick(M, MM_ROWS, BF16_SUBLANE)
    tn = _pick(n_cols, MM_COLS, LANE)
    mt, nt = M // tm, n_cols // tn
    rows = K // mt
    assert K <= MM_DEPTH and K % mt == 0 and rows % BF16_SUBLANE == 0 and n_off % tn == 0
    nb_off = n_off // tn
    row_of = lambda n, m: jnp.where(n > 0, m, 0)
    col_of = lambda n: jnp.maximum(n - 1, 0)
    in_specs = [
        pl.BlockSpec((tm, K), lambda n, m: (row_of(n, m), 0)),
        pl.BlockSpec((None, rows, tn),
                     lambda n, m: (layer, jnp.where(n < nt, m, mt - 1), jnp.minimum(n, nt - 1) + nb_off)),
    ]
    args = [x, w]
    if bias is not None:
        in_specs.append(pl.BlockSpec((None, 1, tn), lambda n, m: (layer, 0, col_of(n) + nb_off)))
        args.append(bias)
    out_shape = [jax.ShapeDtypeStruct((M, n_cols), d) for d in out_dtypes]
    out_specs = [pl.BlockSpec((tm, tn), lambda n, m: (row_of(n, m), col_of(n))) for _ in out_dtypes]
    out_bytes = sum(jnp.dtype(d).itemsize for d in out_dtypes)
    vmem = (2 * tm * K * 2 + 2 * K * tn * 2 + 2 * rows * tn * 4 + 2 * tm * tn * out_bytes
            + 2 * tm * tn * 4 + 4 * MIB)
    if side is not None:
        s_arr, s_idx = side
        R, C = s_arr.shape[1:]
        rs = R // (nt * mt)
        assert R % (nt * mt) == 0 and rs % BF16_SUBLANE == 0
        slab = lambda n, m: jnp.maximum((n - 1) * mt + m, 0)
        in_specs.append(pl.BlockSpec((None, rs, C), lambda n, m: (s_idx, slab(n, m), 0)))
        args.append(s_arr)
        out_shape.append(jax.ShapeDtypeStruct((R, C), BF16))
        out_specs.append(pl.BlockSpec((rs, C), lambda n, m: (slab(n, m), 0)))
        vmem += 2 * rs * C * (4 + 2)
    return pl.pallas_call(
        functools.partial(_mm_ws_kernel, act=act, has_bias=bias is not None, n_out=len(out_dtypes),
                          has_side=side is not None, rows=rows),
        out_shape=out_shape,
        grid=(nt + 1, mt),
        in_specs=in_specs,
        out_specs=out_specs,
        scratch_shapes=[pltpu.VMEM((2, K, tn), BF16)],
        compiler_params=_params(("arbitrary", "arbitrary"), vmem),
        name="matmul_" + act,
    )(*args)


def _resnorm_kernel(*refs, has_next):
    y_ref, m_ref, gpost_ref = refs[0], refs[1], refs[2]
    gnext_ref = refs[3] if has_next else None
    ynew_ref = refs[3 + has_next]
    ynew = y_ref[...] + _rms(m_ref[...], gpost_ref[...])
    ynew_ref[...] = ynew
    if has_next:
        refs[5][...] = _rms(ynew, gnext_ref[...]).astype(BF16)


def _resnorm(y, m, g_post, g_next=None, row_off=0, rows=None):
    D = y.shape[1]
    rows = y.shape[0] if rows is None else rows
    tm = _pick(rows, ROW_TILE, BF16_SUBLANE)
    assert row_off % tm == 0
    ob = row_off // tm
    has_next = g_next is not None
    row = pl.BlockSpec((tm, D), lambda i: (i + ob, 0))
    out_row = pl.BlockSpec((tm, D), lambda i: (i, 0))
    vec = pl.BlockSpec((1, D), lambda i: (0, 0))
    out_shape = [jax.ShapeDtypeStruct((rows, D), F32)]
    if has_next:
        out_shape.append(jax.ShapeDtypeStruct((rows, D), BF16))
    vmem = 2 * tm * D * (4 + 4 + 4 + 2) + 4 * tm * D * 4 + 4 * MIB
    return pl.pallas_call(
        functools.partial(_resnorm_kernel, has_next=has_next),
        out_shape=out_shape,
        grid=(rows // tm,),
        in_specs=[row, row, vec] + ([vec] if has_next else []),
        out_specs=[out_row] * len(out_shape),
        compiler_params=_params(("parallel",), vmem),
        name="resnorm",
    )(*([y, m, g_post] + ([g_next] if has_next else [])))


def _pool_kernel(y_ref, hist_ref, gpre_ref, w_ref, scale_ref, gpost_ref, gnext_ref,
                 ynew_ref, hnext_ref, histout_ref, buf_ref, m_ref, lvl_a, lvl_b, *, nseq, tm, hist_valid):
    @pl.when(pl.program_id(0) < nseq)
    def _():
        _pool_tile(y_ref, hist_ref, gpre_ref, w_ref, scale_ref, gpost_ref, gnext_ref,
                   ynew_ref, hnext_ref, histout_ref, buf_ref, m_ref, lvl_a, lvl_b,
                   tm=tm, hist_valid=hist_valid)

    @pl.when(pl.program_id(0) >= nseq)
    def _():
        ynew_ref[...] = jnp.zeros(ynew_ref.shape, ynew_ref.dtype)
        hnext_ref[...] = jnp.zeros(hnext_ref.shape, hnext_ref.dtype)


def _pool_tile(y_ref, hist_ref, gpre_ref, w_ref, scale_ref, gpost_ref, gnext_ref,
               ynew_ref, hnext_ref, histout_ref, buf_ref, m_ref, lvl_a, lvl_b, *, tm, hist_valid):
    t = pl.program_id(1)
    D = y_ref.shape[-1]
    gw = D // len(POOL_WINDOWS)
    BASE = PAD + HALO
    n = HALO + tm
    for ref in (buf_ref, lvl_a, lvl_b):
        ref[0:PAD, :] = jnp.zeros((PAD, ref.shape[1]), F32)

    @pl.when(t == 0)
    def _():
        buf_ref[PAD:BASE, :] = hist_ref[...]

    @pl.when(t > 0)
    def _():
        buf_ref[PAD:BASE, :] = buf_ref[PAD + tm:BASE + tm, :]

    y = y_ref[...]
    h = _rms(y, gpre_ref[...])
    buf_ref[BASE:BASE + tm, :] = h
    histout_ref[...] = h[tm - HALO:, :]

    def window_sum(cols, w):
        read = lambda lo, rows: buf_ref[lo:lo + rows, cols]
        shift, dst, other = 1, lvl_a, lvl_b
        while 2 * shift < w:
            dst[PAD:PAD + n, :] = read(PAD, n) + read(PAD - shift, n)
            read = lambda lo, rows, src=dst: src[lo:lo + rows, :]
            shift, dst, other = 2 * shift, other, dst
        return read(BASE, tm) + read(BASE - shift, tm)

    pos = t * tm + lax.broadcasted_iota(jnp.int32, (tm, 1), 0)
    ssq = jnp.zeros((tm, 1), F32)
    for g, w in enumerate(POOL_WINDOWS):
        cols = slice(g * gw, (g + 1) * gw)
        hg = buf_ref[BASE:BASE + tm, cols]
        s = window_sum(cols, w)
        if hist_valid:
            mean = s / float(w)
        else:
            mean = s / jnp.minimum(pos + 1, w).astype(F32)
        d = mean - hg
        mg = jnp.dot(d.astype(BF16), w_ref[g], preferred_element_type=F32) * scale_ref[:, cols]
        m_ref[:, cols] = mg
        ssq = ssq + jnp.sum(mg * mg, axis=-1, keepdims=True)
    ynew = y + m_ref[...] * lax.rsqrt(ssq / D + NORM_EPS) * gpost_ref[...]
    ynew_ref[...] = ynew
    hnext_ref[...] = _rms(ynew, gnext_ref[...]).astype(BF16)


def _pool_segment(y, row_off, nseq, T, hist, hist_valid, out_rows, out_off, prev_outs,
                  g_pre, w_bf, scale, g_post, g_next):
    D = y.shape[1]
    G = len(POOL_WINDOWS)
    gw = D // G
    rest = 0 if prev_outs is not None else out_rows - out_off - nseq * T
    tm = _pick(math.gcd(T, rest), ROW_TILE, BF16_SUBLANE)
    nt = T // tm
    assert row_off % tm == 0 and out_off % tm == 0
    ib, ob = row_off // tm, out_off // tm
    fill = rest // tm
    assert 0 <= fill <= nt
    tile_of = lambda s, t: jnp.minimum(s * nt + t, nseq * nt - 1)
    seq_of = lambda s: jnp.minimum(s, nseq - 1)
    vec = pl.BlockSpec((1, D), lambda s, t: (0, 0))
    in_specs = [
        pl.BlockSpec((tm, D), lambda s, t: (ib + tile_of(s, t), 0)),
        pl.BlockSpec((None, HALO, D), lambda s, t: (seq_of(s), 0, 0)),
        vec,
        pl.BlockSpec((G, gw, gw), lambda s, t: (0, 0, 0)),
        vec, vec, vec,
    ]
    args = [y, hist, g_pre, w_bf, scale, g_post, g_next]
    aliases = {}
    if prev_outs is not None:
        in_specs += [pl.BlockSpec(memory_space=pl.ANY)] * 2
        args += list(prev_outs)
        aliases = {7: 0, 8: 1}
    out_row = pl.BlockSpec((tm, D), lambda s, t: (ob + jnp.minimum(s * nt + t, nseq * nt + fill - 1), 0))
    vmem = (2 * tm * D * (4 + 4 + 2) + 2 * G * gw * gw * 2 + 2 * (tm + HALO) * D * 4
            + 3 * tm * D * 4 + 4 * MIB)
    kern = functools.partial(_pool_kernel, nseq=nseq, tm=tm, hist_valid=hist_valid)
    if prev_outs is not None:
        def kern(*refs, _k=kern):
            return _k(*refs[:7], *refs[9:])
    return pl.pallas_call(
        kern,
        out_shape=[jax.ShapeDtypeStruct((out_rows, D), F32),
                   jax.ShapeDtypeStruct((out_rows, D), BF16),
                   jax.ShapeDtypeStruct((nseq, HALO, D), F32)],
        grid=(nseq + (fill > 0), nt),
        in_specs=in_specs,
        out_specs=[out_row, out_row, pl.BlockSpec((None, HALO, D), lambda s, t: (seq_of(s), 0, 0))],
        scratch_shapes=[pltpu.VMEM((PAD + HALO + tm, D), F32), pltpu.VMEM((tm, D), F32),
                        pltpu.VMEM((PAD + HALO + tm, gw), F32), pltpu.VMEM((PAD + HALO + tm, gw), F32)],
        input_output_aliases=aliases,
        compiler_params=_params(("parallel", "arbitrary"), vmem),
        name="pool_mixer",
    )(*args)


def _sgu_gate_kernel(*refs, real, blk, nblk, groups, has_prev, emit_v):
    outs = refs[5 + has_prev:]

    @pl.when(pl.program_id(0) < real)
    def _():
        _sgu_gate_tile(*refs[:5], *outs, blk=blk, nblk=nblk, groups=groups, emit_v=emit_v)

    @pl.when(pl.program_id(0) >= real)
    def _():
        outs[0][...] = jnp.zeros(outs[0].shape, outs[0].dtype)


def _sgu_gate_tile(u_ref, v_ref, gv_ref, ws_ref, bst_ref, out_ref, *rest, blk, nblk, groups, emit_v):
    vout_ref = rest[0] if emit_v else None
    W = u_ref.shape[-1]
    gw = W // groups
    ri = lax.broadcasted_iota(jnp.int32, (SGU_BLOCK, SGU_BLOCK), 0)
    ci = lax.broadcasted_iota(jnp.int32, (SGU_BLOCK, SGU_BLOCK), 1)
    mask = (ci // CHUNK) <= (ri // CHUNK)
    wm = [jnp.where(mask, ws_ref[g], 0.0).astype(BF16)[:blk, :] for g in range(groups)]
    bias = [bst_ref[:blk, g:g + 1] for g in range(groups)]
    for b in range(nblk):
        rows = slice(b * blk, (b + 1) * blk)
        v = _rms(v_ref[rows, :], gv_ref[...])
        if emit_v:
            vout_ref[rows, :] = v
        vb = v.astype(BF16)
        if blk < SGU_BLOCK:
            vb = jnp.concatenate([vb, jnp.zeros((SGU_BLOCK - blk, W), BF16)], axis=0)
        for g in range(groups):
            cols = slice(g * gw, (g + 1) * gw)
            f = jnp.dot(wm[g], vb[:, cols], preferred_element_type=F32) + bias[g]
            out_ref[rows, cols] = (u_ref[rows, cols].astype(F32) * f).astype(BF16)


def _sgu_gate_segment(u, v_raw, row_off, rows, blk, g_v, w_s, b_s_t, prev_out, emit_v):
    M, W = u.shape
    groups = w_s.shape[0]
    rest = 0 if prev_out is not None else M - row_off - rows
    tm = _pick(math.gcd(rows, rest), ROW_TILE, blk)
    nblk = tm // blk
    assert row_off % tm == 0
    ob = row_off // tm
    real, fill = rows // tm, rest // tm
    row = lambda i: (i + ob, 0)
    src = lambda i: (jnp.minimum(i, real - 1) + ob, 0)
    in_specs = [
        pl.BlockSpec((tm, W), src),
        pl.BlockSpec((tm, W), src),
        pl.BlockSpec((1, W), lambda i: (0, 0)),
        pl.BlockSpec((groups, SGU_BLOCK, SGU_BLOCK), lambda i: (0, 0, 0)),
        pl.BlockSpec((SGU_BLOCK, groups), lambda i: (0, 0)),
    ]
    args = [u, v_raw, g_v, w_s, b_s_t]
    aliases = {}
    if prev_out is not None:
        in_specs.append(pl.BlockSpec(memory_space=pl.ANY))
        args.append(prev_out)
        aliases = {5: 0}
    out_shape = [jax.ShapeDtypeStruct((M, W), BF16)]
    out_specs = [pl.BlockSpec((tm, W), row)]
    if emit_v:
        out_shape.append(jax.ShapeDtypeStruct((rows, W), F32))
        out_specs.append(pl.BlockSpec((tm, W), lambda i: (i, 0)))
    vmem = 2 * tm * W * (2 + 4 + 2 + 4) + 4 * tm * W * 4 + 4 * MIB
    return pl.pallas_call(
        functools.partial(_sgu_gate_kernel, real=real, blk=blk, nblk=nblk, groups=groups,
                          has_prev=prev_out is not None, emit_v=emit_v),
        out_shape=out_shape,
        grid=(real + fill,),
        in_specs=in_specs,
        out_specs=out_specs,
        input_output_aliases=aliases,
        compiler_params=_params(("parallel",), vmem),
        name="sgu_gate",
    )(*args)


def _sb_logs(z):
    ls = jnp.minimum(z, 0.0) - jnp.log(1.0 + jnp.exp(-jnp.abs(z)))
    return ls, ls - z


def _suffix_matrix(n):
    j = lax.broadcasted_iota(jnp.int32, (n, n), 0)
    s = lax.broadcasted_iota(jnp.int32, (n, n), 1)
    return jnp.where(j > s, 1.0, 0.0).astype(BF16)


def _suffix_sum(ln, U):
    n = ln.shape[0]
    hi = ln.astype(BF16)
    lo = (ln - hi.astype(F32)).astype(BF16)
    r = jnp.dot(jnp.concatenate([hi, lo], axis=0), U, preferred_element_type=F32)
    return r[:n] + r[n:]


def _sb_live(c_ref):
    c = c_ref[0]
    for g in range(1, c_ref.shape[0]):
        c = jnp.maximum(c, c_ref[g])
    return (jnp.max(c) > SB_SPENT).astype(jnp.int32)


def _sb_tiles(qs, ks, vs, cs, U, scale, allowed=None):
    dn = (((1,), (1,)), ((), ()))
    zs = [lax.dot_general(q, k, dn, preferred_element_type=F32) * scale for q, k in zip(qs, ks)]
    cw = U.shape[0]
    chunks = [slice(j * cw, (j + 1) * cw) for j in range(zs[0].shape[1] // cw)]
    logs = []
    for z in zs:
        ls, ln = _sb_logs(z)
        if allowed is not None:
            ln = jnp.where(allowed, ln, 0.0)
        logs.append((ls, [_suffix_sum(ln[:, cols], U) for cols in chunks],
                     [jnp.sum(ln[:, cols], axis=1, keepdims=True) for cols in chunks]))
    outs, new_cs = [], []
    for (ls, suffix, total), v, c in zip(logs, vs, cs):
        parts = [None] * len(chunks)
        for j in reversed(range(len(chunks))):
            parts[j] = jnp.exp(ls[:, chunks[j]] + suffix[j] + c)
            c = c + total[j]
        a = parts[0] if len(parts) == 1 else jnp.concatenate(parts, axis=1)
        if allowed is not None:
            a = jnp.where(allowed, a, 0.0)
        outs.append(jnp.dot(a.astype(BF16), v, preferred_element_type=F32))
        new_cs.append(c)
    return outs, new_cs


def _sb_prompt_kernel(q_ref, k_ref, v_ref, o_ref, u_ref, c_ref, acc_ref,
                      *, nseq, tq, wide, heads, dh, scale):
    T = q_ref.shape[0]
    cols = [slice(g * dh, (g + 1) * dh) for g in range(heads)]

    def tiles(qs, ks, width, diagonal):
        allowed = None
        if diagonal:
            qi = lax.broadcasted_iota(jnp.int32, (tq, tq), 0)
            ki = lax.broadcasted_iota(jnp.int32, (tq, tq), 1)
            allowed = ki < qi
        outs, cs = _sb_tiles([q_ref[pl.ds(qs, tq), c] for c in cols],
                             [k_ref[pl.ds(ks, width * tq), c] for c in cols],
                             [v_ref[pl.ds(ks, width * tq), c] for c in cols],
                             [jnp.zeros((tq, 1), F32) if diagonal else c_ref[g] for g in range(heads)],
                             u_ref[...], scale, allowed)
        for g in range(heads):
            c_ref[g] = cs[g]
            acc_ref[g] = outs[g] if diagonal else acc_ref[g] + outs[g]

    def q_body(i, carry):
        qs = pl.multiple_of(i * tq, tq)
        tiles(qs, qs, 1, True)
        lead = jnp.where(i > 0, (i - 1) % wide + 1, 0)

        def single_body(st):
            tiles(qs, pl.multiple_of((i - 1 - st[0]) * tq, tq), 1, False)
            return st[0] + 1, _sb_live(c_ref)

        _, live = lax.while_loop(lambda st: jnp.logical_and(st[0] < lead, st[1] > 0), single_body,
                                 (jnp.int32(0), _sb_live(c_ref)))

        def wide_body(st):
            tiles(qs, pl.multiple_of((i - lead - wide * (st[0] + 1)) * tq, tq), wide, False)
            return st[0] + 1, _sb_live(c_ref)

        lax.while_loop(lambda st: jnp.logical_and(st[0] < (i - lead) // wide, st[1] > 0), wide_body,
                       (jnp.int32(0), live))
        for g in range(heads):
            o_ref[pl.ds(qs, tq), cols[g]] = acc_ref[g].astype(o_ref.dtype)
        return carry

    @pl.when(pl.program_id(0) < nseq)
    def _():
        u_ref[...] = _suffix_matrix(tq)
        lax.fori_loop(0, T // tq, q_body, 0)

    @pl.when(pl.program_id(0) >= nseq)
    def _():
        o_ref[...] = jnp.zeros(o_ref.shape, o_ref.dtype)


def _sb_prompt(q, k, v, B, T, H, dh, out_rows):
    tq = _pick(T, SB_QBLOCK, LANE)
    heads = _pick(H, SB_PROMPT_HEADS, 1)
    nb = pl.cdiv(out_rows, T)
    src = pl.BlockSpec((T, heads * dh),
                       lambda b, h: (jnp.minimum(b, B - 1), jnp.where(b < B, h, H // heads - 1)))
    vmem = 2 * 4 * T * heads * dh * 2 + heads * SB_WIDE * 24 * tq * tq * 4 + 4 * MIB
    return pl.pallas_call(
        functools.partial(_sb_prompt_kernel, nseq=B, tq=tq, wide=SB_WIDE, heads=heads, dh=dh,
                          scale=dh ** -0.5),
        out_shape=jax.ShapeDtypeStruct((out_rows, H * dh), BF16),
        grid=(nb, H // heads),
        in_specs=[src, src, src],
        out_specs=pl.BlockSpec((T, heads * dh), lambda b, h: (b, h)),
        scratch_shapes=[pltpu.VMEM((tq, tq), BF16), pltpu.VMEM((heads, tq, 1), F32),
                        pltpu.VMEM((heads, tq, dh), F32)],
        compiler_params=_params(("parallel", "parallel"), vmem),
        name="sb_prompt",
    )(q, k, v)


def _sb_cached_chunk(q_ref, kc_ref, vc_ref, u_ref, c_ref, acc_ref, k_all, v_all, *, heads, scale):
    dh = kc_ref.shape[2]
    cols = [slice(g * dh, (g + 1) * dh) for g in range(heads)]

    @pl.when(_sb_live(c_ref) > 0)
    def _():
        k_all[...] = pltpu.einshape("phd->hpd", kc_ref[...].astype(BF16))
        v_all[...] = pltpu.einshape("phd->hpd", vc_ref[...].astype(BF16))
        for g0 in range(0, heads, SB_SAMPLE_BATCH):
            gs = range(g0, g0 + SB_SAMPLE_BATCH)
            outs, cs = _sb_tiles([q_ref[:, cols[g]] for g in gs], [k_all[g] for g in gs],
                                 [v_all[g] for g in gs], [c_ref[g] for g in gs], u_ref[...], scale)
            for g, o, c in zip(gs, outs, cs):
                c_ref[g] = c
                acc_ref[g] = acc_ref[g] + o


def _sb_sample_head_kernel(q_ref, kn_ref, vn_ref, kc_ref, vc_ref, prev_ref, o_ref, cst_ref, accst_ref,
                           live_ref, u_ref, c_ref, acc_ref, k_all, v_all, *, heads, scale):
    del prev_ref
    Ts = q_ref.shape[0]
    dh = kc_ref.shape[2]
    cols = [slice(g * dh, (g + 1) * dh) for g in range(heads)]
    u_ref[...] = _suffix_matrix(u_ref.shape[0])
    qi = lax.broadcasted_iota(jnp.int32, (Ts, Ts), 0)
    ki = lax.broadcasted_iota(jnp.int32, (Ts, Ts), 1)
    outs, cs = _sb_tiles([q_ref[:, c] for c in cols], [kn_ref[:, c] for c in cols],
                         [vn_ref[:, c] for c in cols], [jnp.zeros((Ts, 1), F32)] * heads,
                         _suffix_matrix(Ts), scale, ki < qi)
    for g in range(heads):
        c_ref[g] = cs[g]
        acc_ref[g] = outs[g]
    _sb_cached_chunk(q_ref, kc_ref, vc_ref, u_ref, c_ref, acc_ref, k_all, v_all, heads=heads, scale=scale)
    for g in range(heads):
        o_ref[:, cols[g]] = acc_ref[g].astype(o_ref.dtype)
    cst_ref[...] = c_ref[...]
    accst_ref[...] = acc_ref[...]
    live_ref[...] = jnp.full(live_ref.shape, _sb_live(c_ref), jnp.int32)


def _sb_sample_tail_kernel(live_ref, q_ref, kc_ref, vc_ref, cst_ref, accst_ref, prev_ref, o_ref,
                           u_ref, c_ref, acc_ref, k_all, v_all, *, heads, scale):
    del live_ref, prev_ref
    pc = pl.program_id(2)
    dh = kc_ref.shape[2]

    @pl.when(pc == 0)
    def _():
        u_ref[...] = _suffix_matrix(u_ref.shape[0])
        c_ref[...] = cst_ref[...]
        acc_ref[...] = accst_ref[...]

    _sb_cached_chunk(q_ref, kc_ref, vc_ref, u_ref, c_ref, acc_ref, k_all, v_all, heads=heads, scale=scale)

    @pl.when(pc == pl.num_programs(2) - 1)
    def _():
        for g in range(heads):
            o_ref[:, g * dh:(g + 1) * dh] = acc_ref[g].astype(o_ref.dtype)


def _sb_sample(q, k, v, cache_k, cache_v, layer, row_off, Bs, Ts, H, dh, prev_out):
    P = cache_k.shape[2]
    heads = F32_SUBLANE
    assert H % heads == 0 and row_off % Ts == 0
    G = H // heads
    cw = _pick(P, SB_CUM, LANE)
    tp = _pick(P, SB_SAMPLE_KEYS, cw)
    npc = P // tp
    ob = row_off // Ts
    scratch = [pltpu.VMEM((cw, cw), BF16), pltpu.VMEM((heads, Ts, 1), F32), pltpu.VMEM((heads, Ts, dh), F32),
               pltpu.VMEM((heads, tp, dh), BF16), pltpu.VMEM((heads, tp, dh), BF16)]
    vmem = 2 * 2 * tp * heads * dh * 4 + heads * 16 * Ts * tp * 4 + 8 * MIB
    kern_args = dict(heads=heads, scale=dh ** -0.5)
    state_shapes = [jax.ShapeDtypeStruct((Bs, G, heads, Ts, 1), F32),
                    jax.ShapeDtypeStruct((Bs, G, heads, Ts, dh), F32)]

    new = pl.BlockSpec((Ts, heads * dh), lambda b, g: (ob + b, g))
    newest = pl.BlockSpec((None, None, tp, heads, dh), lambda b, g: (layer, b, npc - 1, g, 0))
    state = [pl.BlockSpec((None, None, heads, Ts, 1), lambda b, g: (b, g, 0, 0, 0)),
             pl.BlockSpec((None, None, heads, Ts, dh), lambda b, g: (b, g, 0, 0, 0))]
    out, c_state, acc_state, live = pl.pallas_call(
        functools.partial(_sb_sample_head_kernel, **kern_args),
        out_shape=[jax.ShapeDtypeStruct(prev_out.shape, BF16)] + state_shapes
        + [jax.ShapeDtypeStruct((Bs, G, F32_SUBLANE, LANE), jnp.int32)],
        grid=(Bs, G),
        in_specs=[new, new, new, newest, newest, pl.BlockSpec(memory_space=pl.ANY)],
        out_specs=[new] + state + [pl.BlockSpec((None, None, F32_SUBLANE, LANE), lambda b, g: (b, g, 0, 0))],
        scratch_shapes=scratch,
        input_output_aliases={5: 0},
        compiler_params=_params(("parallel", "parallel"), vmem),
        name="sb_sample_head",
    )(q, k, v, cache_k, cache_v, prev_out)
    if npc == 1:
        return out

    def older(b, g, p, live):
        on = live[b, g] > 0
        return (layer, jnp.where(on, b, 0), jnp.where(on, npc - 2 - p, 0), jnp.where(on, g, 0), 0)

    row = lambda b, g, p, live: (ob + b, g)
    st = lambda b, g, p, live: (b, g, 0, 0, 0)
    return pl.pallas_call(
        functools.partial(_sb_sample_tail_kernel, **kern_args),
        out_shape=jax.ShapeDtypeStruct(prev_out.shape, BF16),
        grid_spec=pltpu.PrefetchScalarGridSpec(
            num_scalar_prefetch=1,
            grid=(Bs, G, npc - 1),
            in_specs=[pl.BlockSpec((Ts, heads * dh), row),
                      pl.BlockSpec((None, None, tp, heads, dh), older),
                      pl.BlockSpec((None, None, tp, heads, dh), older),
                      pl.BlockSpec((None, None, heads, Ts, 1), st),
                      pl.BlockSpec((None, None, heads, Ts, dh), st),
                      pl.BlockSpec(memory_space=pl.ANY)],
            out_specs=pl.BlockSpec((Ts, heads * dh), row),
            scratch_shapes=scratch),
        input_output_aliases={6: 0},
        compiler_params=_params(("parallel", "parallel", "arbitrary"), vmem),
        name="sb_sample_tail",
    )(live[:, :, 0, 0], q, cache_k, cache_v, c_state, acc_state, out)


def kernel(x_prompt, x_sample, cache_pool, cache_k, cache_v, g_mix_pre, g_mix_post, g_ffn_pre, g_ffn_post, pool_w, pool_scale, sgu_w_in, sgu_b_in, sgu_g_v, sgu_w_s, sgu_b_s, sgu_w_out, sb_w_qkv, sb_w_o, ffn_w_up, ffn_w_down):
    B, T, D = x_prompt.shape
    Bs, Ts, _ = x_sample.shape
    Mp, Ms = B * T, Bs * Ts
    M = Mp + Ms
    depth = g_mix_pre.shape[0]
    H, dh = cache_k.shape[3], cache_k.shape[4]
    P = cache_k.shape[2]
    keep = min(T, P)
    W = sgu_g_v.shape[1]

    vec = lambda a, i: a[i].reshape(1, -1)
    pool_w_bf = pool_w.astype(BF16)

    zero_hist = jnp.zeros((B, HALO, D), F32)
    pad_hist = lambda c: jnp.pad(c, ((0, 0), (HALO - POOL_HIST, 0), (0, 0)))

    y = None
    h = None
    pool_hist_p, pool_hist_s, sgu_v_s = [], [], []
    sb_k_p, sb_v_p, sb_k_s, sb_v_s = [], [], [], []
    for i in range(depth):
        kind, j = i % 3, i // 3
        g_ffn = vec(g_ffn_pre, i)
        if kind == 0:
            if y is None:
                src_p, off_p, src_s, off_s = x_prompt.reshape(Mp, D), 0, x_sample.reshape(Ms, D), 0
            else:
                src_p, off_p, src_s, off_s = y, 0, y, Mp
            common = (vec(g_mix_pre, i), pool_w_bf[j], vec(pool_scale, j), vec(g_mix_post, i), g_ffn)
            y_p, h_p, hist_p = _pool_segment(src_p, off_p, B, T, zero_hist, False, M, 0, None, *common)
            y, h, hist_s = _pool_segment(src_s, off_s, Bs, Ts, pad_hist(cache_pool[j]), True, M, Mp,
                                         (y_p, h_p), *common)
            pool_hist_p.append(hist_p[:, HALO - POOL_HIST:])
            pool_hist_s.append(hist_s[:, HALO - POOL_HIST:])
        else:
            if kind == 1:
                b_in = sgu_b_in.reshape(sgu_b_in.shape[0], 1, -1)
                (u,) = _matmul(h, sgu_w_in, j, 0, W, [BF16], act="gelu", bias=b_in)
                (v_raw,) = _matmul(h, sgu_w_in, j, W, W, [F32], act="gelu", bias=b_in)
                gate_args = (vec(sgu_g_v, j), sgu_w_s[j], sgu_b_s[j].T)
                (gated,) = _sgu_gate_segment(u, v_raw, 0, Mp, min(T, SGU_BLOCK), *gate_args, None, False)
                gated, v_s = _sgu_gate_segment(u, v_raw, Mp, Ms, min(Ts, SGU_BLOCK), *gate_args, gated, True)
                sgu_v_s.append(v_s.reshape(Bs, Ts, W))
                (m,) = _matmul(gated, sgu_w_out, j, 0, D, [F32])
            else:
                (q,) = _matmul(h, sb_w_qkv, j, 0, D, [BF16])
                k32, k = _matmul(h, sb_w_qkv, j, D, D, [F32, BF16])
                v32, v = _matmul(h, sb_w_qkv, j, 2 * D, D, [F32, BF16])
                o = _sb_prompt(q, k, v, B, T, H, dh, M)
                o = _sb_sample(q, k, v, cache_k, cache_v, j, Mp, Bs, Ts, H, dh, o)
                (m,) = _matmul(o, sb_w_o, j, 0, D, [F32])
                sb_k_p.append(k32[:Mp].reshape(B, T, H, dh)[:, -keep:])
                sb_v_p.append(v32[:Mp].reshape(B, T, H, dh)[:, -keep:])
                sb_k_s.append(k32[Mp:].reshape(Bs, Ts, H, dh))
                sb_v_s.append(v32[Mp:].reshape(Bs, Ts, H, dh))
            y, h = _resnorm(y, m, vec(g_mix_post, i), g_ffn)
        a, w_down_bf = _matmul(h, ffn_w_up, i, 0, ffn_w_up.shape[2], [BF16], act="relu2",
                               side=(ffn_w_down, i))
        m = _matmul_kgrid(a, w_down_bf, F32)
        g_post = vec(g_ffn_post, i)
        if i + 1 == depth:
            (yp,) = _resnorm(y, m, g_post, None, 0, Mp)
            (ys,) = _resnorm(y, m, g_post, None, Mp, Ms)
        elif (i + 1) % 3 == 0:
            (y,) = _resnorm(y, m, g_post)
        else:
            y, h = _resnorm(y, m, g_post, vec(g_mix_pre, i + 1))
    return (yp.reshape(B, T, D), ys.reshape(Bs, Ts, D),
            jnp.stack(pool_hist_p), jnp.stack(pool_hist_s), jnp.stack(sgu_v_s),
            jnp.stack(sb_k_p), jnp.stack(sb_v_p), jnp.stack(sb_k_s), jnp.stack(sb_v_s))
```

```python
import functools
import math

import jax
import jax.numpy as jnp
from jax import lax
from jax.experimental import pallas as pl
from jax.experimental.pallas import tpu as pltpu

F32 = jnp.float32
BF16 = jnp.bfloat16

NORM_EPS = 1e-6
CHUNK = 64
POOL_WINDOWS = (2, 4, 8, 16)
POOL_HIST = 15
HALO = 16
PAD = 8
SGU_BLOCK = 128
ROW_TILE = 256
MM_ROWS = 1088
MM_COLS = 1024
MM_DEPTH = 4096
SB_QBLOCK = 256
SB_CUM = 256
SB_PROMPT_HEADS = 4
SB_WIDE = 4
SB_SPENT = -110.0
SB_SAMPLE_NEWEST = 256
SB_SAMPLE_KEYS = 1280
SB_SAMPLE_BATCH = 8

LANE = 128
F32_SUBLANE = 8
BF16_SUBLANE = 16
VMEM_CAP = 60 * 1024 * 1024
MIB = 1024 * 1024


def _pick(n, cap, mult):
    best = None
    for d in range(mult, min(n, cap) + 1, mult):
        if n % d == 0:
            best = d
    assert best is not None, (n, cap, mult)
    return best


def _params(sem, vmem_bytes):
    return pltpu.CompilerParams(dimension_semantics=sem,
                                vmem_limit_bytes=int(min(VMEM_CAP, vmem_bytes)))


def _rms(x, g):
    ms = jnp.mean(x * x, axis=-1, keepdims=True)
    return x * lax.rsqrt(ms + NORM_EPS) * g


def _activate(acc, act):
    if act == "relu2":
        r = jnp.maximum(acc, 0.0)
        return r * r
    if act == "gelu":
        return 0.5 * acc * (1.0 + lax.erf(acc * (2.0 ** -0.5)))
    return acc


def _mm_kernel(x_ref, w_ref, o_ref, acc_ref, *, nk):
    k = pl.program_id(2)
    dot = lambda: jnp.dot(x_ref[...], w_ref[...], preferred_element_type=F32)
    if nk == 1:
        o_ref[...] = dot().astype(o_ref.dtype)
        return

    @pl.when(k == 0)
    def _():
        acc_ref[...] = dot()

    @pl.when(jnp.logical_and(k > 0, k < nk - 1))
    def _():
        acc_ref[...] += dot()

    @pl.when(k == nk - 1)
    def _():
        o_ref[...] = (acc_ref[...] + dot()).astype(o_ref.dtype)


def _matmul_kgrid(x, w, out_dtype):
    M, K = x.shape
    N = w.shape[1]
    tm = _pick(M, MM_ROWS, BF16_SUBLANE)
    tn = _pick(N, MM_COLS, LANE)
    tk = _pick(K, MM_DEPTH, LANE)
    nk = K // tk
    vmem = (2 * tm * tk * 2 + 2 * tk * tn * 2 + 2 * tm * tn * jnp.dtype(out_dtype).itemsize
            + 3 * tm * tn * 4 + 4 * MIB)
    return pl.pallas_call(
        functools.partial(_mm_kernel, nk=nk),
        out_shape=jax.ShapeDtypeStruct((M, N), out_dtype),
        grid=(N // tn, M // tm, nk),
        in_specs=[pl.BlockSpec((tm, tk), lambda n, m, k: (m, k)),
                  pl.BlockSpec((tk, tn), lambda n, m, k: (k, n))],
        out_specs=pl.BlockSpec((tm, tn), lambda n, m, k: (m, n)),
        scratch_shapes=[pltpu.VMEM((tm, tn), F32)],
        compiler_params=_params(("parallel", "parallel", "arbitrary"), vmem),
        name="matmul_kgrid",
    )(x, w)


def _mm_ws_kernel(*refs, act, has_bias, n_out, has_side, rows):
    x_ref, w_ref = refs[0], refs[1]
    pos = 2
    b_ref = refs[pos] if has_bias else None
    pos += has_bias
    side_in = refs[pos] if has_side else None
    pos += has_side
    outs = refs[pos:pos + n_out]
    pos += n_out
    side_out = refs[pos] if has_side else None
    pos += has_side
    wbf_ref = refs[pos]
    n, m = pl.program_id(0), pl.program_id(1)
    last = pl.num_programs(0) - 1

    def multiply():
        acc = jnp.dot(x_ref[...], wbf_ref[(n - 1) % 2], preferred_element_type=F32)
        if has_bias:
            acc = acc + b_ref[...]
        acc = _activate(acc, act)
        for o in outs:
            o[...] = acc.astype(o.dtype)
        if has_side:
            side_out[...] = side_in[...].astype(side_out.dtype)

    def cast_next():
        r0 = pl.multiple_of(m * rows, rows)
        wbf_ref[n % 2, pl.ds(r0, rows), :] = w_ref[...].astype(BF16)

    @pl.when(n == 0)
    def _():
        cast_next()

    @pl.when(jnp.logical_and(n > 0, n < last))
    def _():
        multiply()
        cast_next()

    @pl.when(n == last)
    def _():
        multiply()


def _matmul(x, w, layer, n_off, n_cols, out_dtypes, act="none", bias=None, side=None):
    M, K = x.shape
    tm = _pick(M, MM_ROWS, BF16_SUBLANE)
    tn = _pick(n_cols, MM_COLS, LANE)
    mt, nt = M // tm, n_cols // tn
    rows = K // mt
    assert K <= MM_DEPTH and K % mt == 0 and rows % BF16_SUBLANE == 0 and n_off % tn == 0
    nb_off = n_off // tn
    row_of = lambda n, m: jnp.where(n > 0, m, 0)
    col_of = lambda n: jnp.maximum(n - 1, 0)
    in_specs = [
        pl.BlockSpec((tm, K), lambda n, m: (row_of(n, m), 0)),
        pl.BlockSpec((None, rows, tn),
                     lambda n, m: (layer, jnp.where(n < nt, m, mt - 1), jnp.minimum(n, nt - 1) + nb_off)),
    ]
    args = [x, w]
    if bias is not None:
        in_specs.append(pl.BlockSpec((None, 1, tn), lambda n, m: (layer, 0, col_of(n) + nb_off)))
        args.append(bias)
    out_shape = [jax.ShapeDtypeStruct((M, n_cols), d) for d in out_dtypes]
    out_specs = [pl.BlockSpec((tm, tn), lambda n, m: (row_of(n, m), col_of(n))) for _ in out_dtypes]
    out_bytes = sum(jnp.dtype(d).itemsize for d in out_dtypes)
    vmem = (2 * tm * K * 2 + 2 * K * tn * 2 + 2 * rows * tn * 4 + 2 * tm * tn * out_bytes
            + 2 * tm * tn * 4 + 4 * MIB)
    if side is not None:
        s_arr, s_idx = side
        R, C = s_arr.shape[1:]
        rs = R // (nt * mt)
        assert R % (nt * mt) == 0 and rs % BF16_SUBLANE == 0
        slab = lambda n, m: jnp.maximum((n - 1) * mt + m, 0)
        in_specs.append(pl.BlockSpec((None, rs, C), lambda n, m: (s_idx, slab(n, m), 0)))
        args.append(s_arr)
        out_shape.append(jax.ShapeDtypeStruct((R, C), BF16))
        out_specs.append(pl.BlockSpec((rs, C), lambda n, m: (slab(n, m), 0)))
        vmem += 2 * rs * C * (4 + 2)
    return pl.pallas_call(
        functools.partial(_mm_ws_kernel, act=act, has_bias=bias is not None, n_out=len(out_dtypes),
                          has_side=side is not None, rows=rows),
        out_shape=out_shape,
        grid=(nt + 1, mt),
        in_specs=in_specs,
        out_specs=out_specs,
        scratch_shapes=[pltpu.VMEM((2, K, tn), BF16)],
        compiler_params=_params(("arbitrary", "arbitrary"), vmem),
        name="matmul_" + act,
    )(*args)


def _resnorm_kernel(*refs, has_next):
    y_ref, m_ref, gpost_ref = refs[0], refs[1], refs[2]
    gnext_ref = refs[3] if has_next else None
    ynew_ref = refs[3 + has_next]
    ynew = y_ref[...] + _rms(m_ref[...], gpost_ref[...])
    ynew_ref[...] = ynew
    if has_next:
        refs[5][...] = _rms(ynew, gnext_ref[...]).astype(BF16)


def _resnorm(y, m, g_post, g_next=None, row_off=0, rows=None):
    D = y.shape[1]
    rows = y.shape[0] if rows is None else rows
    tm = _pick(rows, ROW_TILE, BF16_SUBLANE)
    assert row_off % tm == 0
    ob = row_off // tm
    has_next = g_next is not None
    row = pl.BlockSpec((tm, D), lambda i: (i + ob, 0))
    out_row = pl.BlockSpec((tm, D), lambda i: (i, 0))
    vec = pl.BlockSpec((1, D), lambda i: (0, 0))
    out_shape = [jax.ShapeDtypeStruct((rows, D), F32)]
    if has_next:
        out_shape.append(jax.ShapeDtypeStruct((rows, D), BF16))
    vmem = 2 * tm * D * (4 + 4 + 4 + 2) + 4 * tm * D * 4 + 4 * MIB
    return pl.pallas_call(
        functools.partial(_resnorm_kernel, has_next=has_next),
        out_shape=out_shape,
        grid=(rows // tm,),
        in_specs=[row, row, vec] + ([vec] if has_next else []),
        out_specs=[out_row] * len(out_shape),
        compiler_params=_params(("parallel",), vmem),
        name="resnorm",
    )(*([y, m, g_post] + ([g_next] if has_next else [])))


def _pool_kernel(y_ref, hist_ref, gpre_ref, w_ref, scale_ref, gpost_ref, gnext_ref,
                 ynew_ref, hnext_ref, histout_ref, buf_ref, m_ref, lvl_a, lvl_b, *, nseq, tm, hist_valid):
    @pl.when(pl.program_id(0) < nseq)
    def _():
        _pool_tile(y_ref, hist_ref, gpre_ref, w_ref, scale_ref, gpost_ref, gnext_ref,
                   ynew_ref, hnext_ref, histout_ref, buf_ref, m_ref, lvl_a, lvl_b,
                   tm=tm, hist_valid=hist_valid)

    @pl.when(pl.program_id(0) >= nseq)
    def _():
        ynew_ref[...] = jnp.zeros(ynew_ref.shape, ynew_ref.dtype)
        hnext_ref[...] = jnp.zeros(hnext_ref.shape, hnext_ref.dtype)


def _pool_tile(y_ref, hist_ref, gpre_ref, w_ref, scale_ref, gpost_ref, gnext_ref,
               ynew_ref, hnext_ref, histout_ref, buf_ref, m_ref, lvl_a, lvl_b, *, tm, hist_valid):
    t = pl.program_id(1)
    D = y_ref.shape[-1]
    gw = D // len(POOL_WINDOWS)
    BASE = PAD + HALO
    n = HALO + tm
    for ref in (buf_ref, lvl_a, lvl_b):
        ref[0:PAD, :] = jnp.zeros((PAD, ref.shape[1]), F32)

    @pl.when(t == 0)
    def _():
        buf_ref[PAD:BASE, :] = hist_ref[...]

    @pl.when(t > 0)
    def _():
        buf_ref[PAD:BASE, :] = buf_ref[PAD + tm:BASE + tm, :]

    y = y_ref[...]
    h = _rms(y, gpre_ref[...])
    buf_ref[BASE:BASE + tm, :] = h
    histout_ref[...] = h[tm - HALO:, :]

    def window_sum(cols, w):
        read = lambda lo, rows: buf_ref[lo:lo + rows, cols]
        shift, dst, other = 1, lvl_a, lvl_b
        while 2 * shift < w:
            dst[PAD:PAD + n, :] = read(PAD, n) + read(PAD - shift, n)
            read = lambda lo, rows, src=dst: src[lo:lo + rows, :]
            shift, dst, other = 2 * shift, other, dst
        return read(BASE, tm) + read(BASE - shift, tm)

    pos = t * tm + lax.broadcasted_iota(jnp.int32, (tm, 1), 0)
    ssq = jnp.zeros((tm, 1), F32)
    for g, w in enumerate(POOL_WINDOWS):
        cols = slice(g * gw, (g + 1) * gw)
        hg = buf_ref[BASE:BASE + tm, cols]
        s = window_sum(cols, w)
        if hist_valid:
            mean = s / float(w)
        else:
            mean = s / jnp.minimum(pos + 1, w).astype(F32)
        d = mean - hg
        mg = jnp.dot(d.astype(BF16), w_ref[g], preferred_element_type=F32) * scale_ref[:, cols]
        m_ref[:, cols] = mg
        ssq = ssq + jnp.sum(mg * mg, axis=-1, keepdims=True)
    ynew = y + m_ref[...] * lax.rsqrt(ssq / D + NORM_EPS) * gpost_ref[...]
    ynew_ref[...] = ynew
    hnext_ref[...] = _rms(ynew, gnext_ref[...]).astype(BF16)


def _pool_segment(y, row_off, nseq, T, hist, hist_valid, out_rows, out_off, prev_outs,
                  g_pre, w_bf, scale, g_post, g_next):
    D = y.shape[1]
    G = len(POOL_WINDOWS)
    gw = D // G
    rest = 0 if prev_outs is not None else out_rows - out_off - nseq * T
    tm = _pick(math.gcd(T, rest), ROW_TILE, BF16_SUBLANE)
    nt = T // tm
    assert row_off % tm == 0 and out_off % tm == 0
    ib, ob = row_off // tm, out_off // tm
    fill = rest // tm
    assert 0 <= fill <= nt
    tile_of = lambda s, t: jnp.minimum(s * nt + t, nseq * nt - 1)
    seq_of = lambda s: jnp.minimum(s, nseq - 1)
    vec = pl.BlockSpec((1, D), lambda s, t: (0, 0))
    in_specs = [
        pl.BlockSpec((tm, D), lambda s, t: (ib + tile_of(s, t), 0)),
        pl.BlockSpec((None, HALO, D), lambda s, t: (seq_of(s), 0, 0)),
        vec,
        pl.BlockSpec((G, gw, gw), lambda s, t: (0, 0, 0)),
        vec, vec, vec,
    ]
    args = [y, hist, g_pre, w_bf, scale, g_post, g_next]
    aliases = {}
    if prev_outs is not None:
        in_specs += [pl.BlockSpec(memory_space=pl.ANY)] * 2
        args += list(prev_outs)
        aliases = {7: 0, 8: 1}
    out_row = pl.BlockSpec((tm, D), lambda s, t: (ob + jnp.minimum(s * nt + t, nseq * nt + fill - 1), 0))
    vmem = (2 * tm * D * (4 + 4 + 2) + 2 * G * gw * gw * 2 + 2 * (tm + HALO) * D * 4
            + 3 * tm * D * 4 + 4 * MIB)
    kern = functools.partial(_pool_kernel, nseq=nseq, tm=tm, hist_valid=hist_valid)
    if prev_outs is not None:
        def kern(*refs, _k=kern):
            return _k(*refs[:7], *refs[9:])
    return pl.pallas_call(
        kern,
        out_shape=[jax.ShapeDtypeStruct((out_rows, D), F32),
                   jax.ShapeDtypeStruct((out_rows, D), BF16),
                   jax.ShapeDtypeStruct((nseq, HALO, D), F32)],
        grid=(nseq + (fill > 0), nt),
        in_specs=in_specs,
        out_specs=[out_row, out_row, pl.BlockSpec((None, HALO, D), lambda s, t: (seq_of(s), 0, 0))],
        scratch_shapes=[pltpu.VMEM((PAD + HALO + tm, D), F32), pltpu.VMEM((tm, D), F32),
                        pltpu.VMEM((PAD + HALO + tm, gw), F32), pltpu.VMEM((PAD + HALO + tm, gw), F32)],
        input_output_aliases=aliases,
        compiler_params=_params(("parallel", "arbitrary"), vmem),
        name="pool_mixer",
    )(*args)


def _sgu_gate_kernel(*refs, real, blk, nblk, groups, has_prev, emit_v):
    outs = refs[5 + has_prev:]

    @pl.when(pl.program_id(0) < real)
    def _():
        _sgu_gate_tile(*refs[:5], *outs, blk=blk, nblk=nblk, groups=groups, emit_v=emit_v)

    @pl.when(pl.program_id(0) >= real)
    def _():
        outs[0][...] = jnp.zeros(outs[0].shape, outs[0].dtype)


def _sgu_gate_tile(u_ref, v_ref, gv_ref, ws_ref, bst_ref, out_ref, *rest, blk, nblk, groups, emit_v):
    vout_ref = rest[0] if emit_v else None
    W = u_ref.shape[-1]
    gw = W // groups
    ri = lax.broadcasted_iota(jnp.int32, (SGU_BLOCK, SGU_BLOCK), 0)
    ci = lax.broadcasted_iota(jnp.int32, (SGU_BLOCK, SGU_BLOCK), 1)
    mask = (ci // CHUNK) <= (ri // CHUNK)
    wm = [jnp.where(mask, ws_ref[g], 0.0).astype(BF16)[:blk, :] for g in range(groups)]
    bias = [bst_ref[:blk, g:g + 1] for g in range(groups)]
    for b in range(nblk):
        rows = slice(b * blk, (b + 1) * blk)
        v = _rms(v_ref[rows, :], gv_ref[...])
        if emit_v:
            vout_ref[rows, :] = v
        vb = v.astype(BF16)
        if blk < SGU_BLOCK:
            vb = jnp.concatenate([vb, jnp.zeros((SGU_BLOCK - blk, W), BF16)], axis=0)
        for g in range(groups):
            cols = slice(g * gw, (g + 1) * gw)
            f = jnp.dot(wm[g], vb[:, cols], preferred_element_type=F32) + bias[g]
            out_ref[rows, cols] = (u_ref[rows, cols].astype(F32) * f).astype(BF16)


def _sgu_gate_segment(u, v_raw, row_off, rows, blk, g_v, w_s, b_s_t, prev_out, emit_v):
    M, W = u.shape
    groups = w_s.shape[0]
    rest = 0 if prev_out is not None else M - row_off - rows
    tm = _pick(math.gcd(rows, rest), ROW_TILE, blk)
    nblk = tm // blk
    assert row_off % tm == 0
    ob = row_off // tm
    real, fill = rows // tm, rest // tm
    row = lambda i: (i + ob, 0)
    src = lambda i: (jnp.minimum(i, real - 1) + ob, 0)
    in_specs = [
        pl.BlockSpec((tm, W), src),
        pl.BlockSpec((tm, W), src),
        pl.BlockSpec((1, W), lambda i: (0, 0)),
        pl.BlockSpec((groups, SGU_BLOCK, SGU_BLOCK), lambda i: (0, 0, 0)),
        pl.BlockSpec((SGU_BLOCK, groups), lambda i: (0, 0)),
    ]
    args = [u, v_raw, g_v, w_s, b_s_t]
    aliases = {}
    if prev_out is not None:
        in_specs.append(pl.BlockSpec(memory_space=pl.ANY))
        args.append(prev_out)
        aliases = {5: 0}
    out_shape = [jax.ShapeDtypeStruct((M, W), BF16)]
    out_specs = [pl.BlockSpec((tm, W), row)]
    if emit_v:
        out_shape.append(jax.ShapeDtypeStruct((rows, W), F32))
        out_specs.append(pl.BlockSpec((tm, W), lambda i: (i, 0)))
    vmem = 2 * tm * W * (2 + 4 + 2 + 4) + 4 * tm * W * 4 + 4 * MIB
    return pl.pallas_call(
        functools.partial(_sgu_gate_kernel, real=real, blk=blk, nblk=nblk, groups=groups,
                          has_prev=prev_out is not None, emit_v=emit_v),
        out_shape=out_shape,
        grid=(real + fill,),
        in_specs=in_specs,
        out_specs=out_specs,
        input_output_aliases=aliases,
        compiler_params=_params(("parallel",), vmem),
        name="sgu_gate",
    )(*args)


def _sb_logs(z):
    ls = jnp.minimum(z, 0.0) - jnp.log(1.0 + jnp.exp(-jnp.abs(z)))
    return ls, ls - z


def _suffix_matrix(n):
    j = lax.broadcasted_iota(jnp.int32, (n, n), 0)
    s = lax.broadcasted_iota(jnp.int32, (n, n), 1)
    return jnp.where(j > s, 1.0, 0.0).astype(BF16)


def _suffix_sum(ln, U):
    n = ln.shape[0]
    hi = ln.astype(BF16)
    lo = (ln - hi.astype(F32)).astype(BF16)
    r = jnp.dot(jnp.concatenate([hi, lo], axis=0), U, preferred_element_type=F32)
    return r[:n] + r[n:]


def _sb_live(c_ref):
    c = c_ref[0]
    for g in range(1, c_ref.shape[0]):
        c = jnp.maximum(c, c_ref[g])
    return (jnp.max(c) > SB_SPENT).astype(jnp.int32)


def _sb_tiles(qs, ks, vs, cs, U, scale, allowed=None):
    dn = (((1,), (1,)), ((), ()))
    zs = [lax.dot_general(q, k, dn, preferred_element_type=F32) * scale for q, k in zip(qs, ks)]
    cw = U.shape[0]
    chunks = [slice(j * cw, (j + 1) * cw) for j in range(zs[0].shape[1] // cw)]
    logs = []
    for z in zs:
        ls, ln = _sb_logs(z)
        if allowed is not None:
            ln = jnp.where(allowed, ln, 0.0)
        logs.append((ls, [_suffix_sum(ln[:, cols], U) for cols in chunks],
                     [jnp.sum(ln[:, cols], axis=1, keepdims=True) for cols in chunks]))
    outs, new_cs = [], []
    for (ls, suffix, total), v, c in zip(logs, vs, cs):
        parts = [None] * len(chunks)
        for j in reversed(range(len(chunks))):
            parts[j] = jnp.exp(ls[:, chunks[j]] + suffix[j] + c)
            c = c + total[j]
        a = parts[0] if len(parts) == 1 else jnp.concatenate(parts, axis=1)
        if allowed is not None:
            a = jnp.where(allowed, a, 0.0)
        outs.append(jnp.dot(a.astype(BF16), v, preferred_element_type=F32))
        new_cs.append(c)
    return outs, new_cs


def _sb_prompt_kernel(q_ref, k_ref, v_ref, o_ref, u_ref, c_ref, acc_ref,
                      *, nseq, tq, wide, heads, dh, scale):
    T = q_ref.shape[0]
    cols = [slice(g * dh, (g + 1) * dh) for g in range(heads)]

    def tiles(qs, ks, width, diagonal):
        allowed = None
        if diagonal:
            qi = lax.broadcasted_iota(jnp.int32, (tq, tq), 0)
            ki = lax.broadcasted_iota(jnp.int32, (tq, tq), 1)
            allowed = ki < qi
        outs, cs = _sb_tiles([q_ref[pl.ds(qs, tq), c] for c in cols],
                             [k_ref[pl.ds(ks, width * tq), c] for c in cols],
                             [v_ref[pl.ds(ks, width * tq), c] for c in cols],
                             [jnp.zeros((tq, 1), F32) if diagonal else c_ref[g] for g in range(heads)],
                             u_ref[...], scale, allowed)
        for g in range(heads):
            c_ref[g] = cs[g]
            acc_ref[g] = outs[g] if diagonal else acc_ref[g] + outs[g]

    def q_body(i, carry):
        qs = pl.multiple_of(i * tq, tq)
        tiles(qs, qs, 1, True)
        lead = jnp.where(i > 0, (i - 1) % wide + 1, 0)

        def single_body(st):
            tiles(qs, pl.multiple_of((i - 1 - st[0]) * tq, tq), 1, False)
            return st[0] + 1, _sb_live(c_ref)

        _, live = lax.while_loop(lambda st: jnp.logical_and(st[0] < lead, st[1] > 0), single_body,
                                 (jnp.int32(0), _sb_live(c_ref)))

        def wide_body(st):
            tiles(qs, pl.multiple_of((i - lead - wide * (st[0] + 1)) * tq, tq), wide, False)
            return st[0] + 1, _sb_live(c_ref)

        lax.while_loop(lambda st: jnp.logical_and(st[0] < (i - lead) // wide, st[1] > 0), wide_body,
                       (jnp.int32(0), live))
        for g in range(heads):
            o_ref[pl.ds(qs, tq), cols[g]] = acc_ref[g].astype(o_ref.dtype)
        return carry

    @pl.when(pl.program_id(0) < nseq)
    def _():
        u_ref[...] = _suffix_matrix(tq)
        lax.fori_loop(0, T // tq, q_body, 0)

    @pl.when(pl.program_id(0) >= nseq)
    def _():
        o_ref[...] = jnp.zeros(o_ref.shape, o_ref.dtype)


def _sb_prompt(q, k, v, B, T, H, dh, out_rows):
    tq = _pick(T, SB_QBLOCK, LANE)
    heads = _pick(H, SB_PROMPT_HEADS, 1)
    nb = pl.cdiv(out_rows, T)
    src = pl.BlockSpec((T, heads * dh),
                       lambda b, h: (jnp.minimum(b, B - 1), jnp.where(b < B, h, H // heads - 1)))
    vmem = 2 * 4 * T * heads * dh * 2 + heads * SB_WIDE * 24 * tq * tq * 4 + 4 * MIB
    return pl.pallas_call(
        functools.partial(_sb_prompt_kernel, nseq=B, tq=tq, wide=SB_WIDE, heads=heads, dh=dh,
                          scale=dh ** -0.5),
        out_shape=jax.ShapeDtypeStruct((out_rows, H * dh), BF16),
        grid=(nb, H // heads),
        in_specs=[src, src, src],
        out_specs=pl.BlockSpec((T, heads * dh), lambda b, h: (b, h)),
        scratch_shapes=[pltpu.VMEM((tq, tq), BF16), pltpu.VMEM((heads, tq, 1), F32),
                        pltpu.VMEM((heads, tq, dh), F32)],
        compiler_params=_params(("parallel", "parallel"), vmem),
        name="sb_prompt",
    )(q, k, v)


def _sb_cached_chunk(q_ref, kc_ref, vc_ref, u_ref, c_ref, acc_ref, k_all, v_all, *, heads, scale):
    dh = kc_ref.shape[2]
    cols = [slice(g * dh, (g + 1) * dh) for g in range(heads)]

    @pl.when(_sb_live(c_ref) > 0)
    def _():
        k_all[...] = pltpu.einshape("phd->hpd", kc_ref[...].astype(BF16))
        v_all[...] = pltpu.einshape("phd->hpd", vc_ref[...].astype(BF16))
        for g0 in range(0, heads, SB_SAMPLE_BATCH):
            gs = range(g0, g0 + SB_SAMPLE_BATCH)
            outs, cs = _sb_tiles([q_ref[:, cols[g]] for g in gs], [k_all[g] for g in gs],
                                 [v_all[g] for g in gs], [c_ref[g] for g in gs], u_ref[...], scale)
            for g, o, c in zip(gs, outs, cs):
                c_ref[g] = c
                acc_ref[g] = acc_ref[g] + o


def _sb_sample_head_kernel(q_ref, kn_ref, vn_ref, kc_ref, vc_ref, prev_ref, o_ref, cst_ref, accst_ref,
                           live_ref, u_ref, c_ref, acc_ref, k_all, v_all, *, heads, scale):
    del prev_ref
    Ts = q_ref.shape[0]
    dh = kc_ref.shape[2]
    cols = [slice(g * dh, (g + 1) * dh) for g in range(heads)]
    u_ref[...] = _suffix_matrix(u_ref.shape[0])
    qi = lax.broadcasted_iota(jnp.int32, (Ts, Ts), 0)
    ki = lax.broadcasted_iota(jnp.int32, (Ts, Ts), 1)
    outs, cs = _sb_tiles([q_ref[:, c] for c in cols], [kn_ref[:, c] for c in cols],
                         [vn_ref[:, c] for c in cols], [jnp.zeros((Ts, 1), F32)] * heads,
                         _suffix_matrix(Ts), scale, ki < qi)
    for g in range(heads):
        c_ref[g] = cs[g]
        acc_ref[g] = outs[g]
    _sb_cached_chunk(q_ref, kc_ref, vc_ref, u_ref, c_ref, acc_ref, k_all, v_all, heads=heads, scale=scale)
    for g in range(heads):
        o_ref[:, cols[g]] = acc_ref[g].astype(o_ref.dtype)
    cst_ref[...] = c_ref[...]
    accst_ref[...] = acc_ref[...]
    live_ref[...] = jnp.full(live_ref.shape, _sb_live(c_ref), jnp.int32)


def _sb_sample_tail_kernel(live_ref, q_ref, kc_ref, vc_ref, cst_ref, accst_ref, prev_ref, o_ref,
                           u_ref, c_ref, acc_ref, k_all, v_all, *, heads, scale):
    del live_ref, prev_ref
    pc = pl.program_id(2)
    dh = kc_ref.shape[2]

    @pl.when(pc == 0)
    def _():
        u_ref[...] = _suffix_matrix(u_ref.shape[0])
        c_ref[...] = cst_ref[...]
        acc_ref[...] = accst_ref[...]

    _sb_cached_chunk(q_ref, kc_ref, vc_ref, u_ref, c_ref, acc_ref, k_all, v_all, heads=heads, scale=scale)

    @pl.when(pc == pl.num_programs(2) - 1)
    def _():
        for g in range(heads):
            o_ref[:, g * dh:(g + 1) * dh] = acc_ref[g].astype(o_ref.dtype)


def _sb_sample(q, k, v, cache_k, cache_v, layer, row_off, Bs, Ts, H, dh, prev_out):
    P = cache_k.shape[2]
    heads = F32_SUBLANE
    assert H % heads == 0 and row_off % Ts == 0
    G = H // heads
    cw = _pick(P, SB_CUM, LANE)
    tph = _pick(P, SB_SAMPLE_NEWEST, cw)
    rest = P - tph
    ob = row_off // Ts
    scratch = lambda tp: [pltpu.VMEM((cw, cw), BF16), pltpu.VMEM((heads, Ts, 1), F32),
                          pltpu.VMEM((heads, Ts, dh), F32),
                          pltpu.VMEM((heads, tp, dh), BF16), pltpu.VMEM((heads, tp, dh), BF16)]
    vmem = lambda tp: 2 * 2 * tp * heads * dh * 4 + heads * 16 * Ts * tp * 4 + 8 * MIB
    kern_args = dict(heads=heads, scale=dh ** -0.5)
    state_shapes = [jax.ShapeDtypeStruct((Bs, G, heads, Ts, 1), F32),
                    jax.ShapeDtypeStruct((Bs, G, heads, Ts, dh), F32)]

    new = pl.BlockSpec((Ts, heads * dh), lambda b, g: (ob + b, g))
    newest = pl.BlockSpec((None, None, tph, heads, dh), lambda b, g: (layer, b, P // tph - 1, g, 0))
    state = [pl.BlockSpec((None, None, heads, Ts, 1), lambda b, g: (b, g, 0, 0, 0)),
             pl.BlockSpec((None, None, heads, Ts, dh), lambda b, g: (b, g, 0, 0, 0))]
    out, c_state, acc_state, live = pl.pallas_call(
        functools.partial(_sb_sample_head_kernel, **kern_args),
        out_shape=[jax.ShapeDtypeStruct(prev_out.shape, BF16)] + state_shapes
        + [jax.ShapeDtypeStruct((Bs, G, F32_SUBLANE, LANE), jnp.int32)],
        grid=(Bs, G),
        in_specs=[new, new, new, newest, newest, pl.BlockSpec(memory_space=pl.ANY)],
        out_specs=[new] + state + [pl.BlockSpec((None, None, F32_SUBLANE, LANE), lambda b, g: (b, g, 0, 0))],
        scratch_shapes=scratch(tph),
        input_output_aliases={5: 0},
        compiler_params=_params(("parallel", "parallel"), vmem(tph)),
        name="sb_sample_head",
    )(q, k, v, cache_k, cache_v, prev_out)
    if rest == 0:
        return out
    tp = _pick(rest, SB_SAMPLE_KEYS, cw)
    npc = rest // tp

    def older(b, g, p, live):
        on = live[b, g] > 0
        return (layer, jnp.where(on, b, 0), jnp.where(on, npc - 1 - p, 0), jnp.where(on, g, 0), 0)

    row = lambda b, g, p, live: (ob + b, g)
    st = lambda b, g, p, live: (b, g, 0, 0, 0)
    return pl.pallas_call(
        functools.partial(_sb_sample_tail_kernel, **kern_args),
        out_shape=jax.ShapeDtypeStruct(prev_out.shape, BF16),
        grid_spec=pltpu.PrefetchScalarGridSpec(
            num_scalar_prefetch=1,
            grid=(Bs, G, npc),
            in_specs=[pl.BlockSpec((Ts, heads * dh), row),
                      pl.BlockSpec((None, None, tp, heads, dh), older),
                      pl.BlockSpec((None, None, tp, heads, dh), older),
                      pl.BlockSpec((None, None, heads, Ts, 1), st),
                      pl.BlockSpec((None, None, heads, Ts, dh), st),
                      pl.BlockSpec(memory_space=pl.ANY)],
            out_specs=pl.BlockSpec((Ts, heads * dh), row),
            scratch_shapes=scratch(tp)),
        input_output_aliases={6: 0},
        compiler_params=_params(("parallel", "parallel", "arbitrary"), vmem(tp)),
        name="sb_sample_tail",
    )(live[:, :, 0, 0], q, cache_k, cache_v, c_state, acc_state, out)


def kernel(x_prompt, x_sample, cache_pool, cache_k, cache_v, g_mix_pre, g_mix_post, g_ffn_pre, g_ffn_post, pool_w, pool_scale, sgu_w_in, sgu_b_in, sgu_g_v, sgu_w_s, sgu_b_s, sgu_w_out, sb_w_qkv, sb_w_o, ffn_w_up, ffn_w_down):
    B, T, D = x_prompt.shape
    Bs, Ts, _ = x_sample.shape
    Mp, Ms = B * T, Bs * Ts
    M = Mp + Ms
    depth = g_mix_pre.shape[0]
    H, dh = cache_k.shape[3], cache_k.shape[4]
    P = cache_k.shape[2]
    keep = min(T, P)
    W = sgu_g_v.shape[1]

    vec = lambda a, i: a[i].reshape(1, -1)
    pool_w_bf = pool_w.astype(BF16)

    zero_hist = jnp.zeros((B, HALO, D), F32)
    pad_hist = lambda c: jnp.pad(c, ((0, 0), (HALO - POOL_HIST, 0), (0, 0)))

    y = None
    h = None
    pool_hist_p, pool_hist_s, sgu_v_s = [], [], []
    sb_k_p, sb_v_p, sb_k_s, sb_v_s = [], [], [], []
    for i in range(depth):
        kind, j = i % 3, i // 3
        g_ffn = vec(g_ffn_pre, i)
        if kind == 0:
            if y is None:
                src_p, off_p, src_s, off_s = x_prompt.reshape(Mp, D), 0, x_sample.reshape(Ms, D), 0
            else:
                src_p, off_p, src_s, off_s = y, 0, y, Mp
            common = (vec(g_mix_pre, i), pool_w_bf[j], vec(pool_scale, j), vec(g_mix_post, i), g_ffn)
            y_p, h_p, hist_p = _pool_segment(src_p, off_p, B, T, zero_hist, False, M, 0, None, *common)
            y, h, hist_s = _pool_segment(src_s, off_s, Bs, Ts, pad_hist(cache_pool[j]), True, M, Mp,
                                         (y_p, h_p), *common)
            pool_hist_p.append(hist_p[:, HALO - POOL_HIST:])
            pool_hist_s.append(hist_s[:, HALO - POOL_HIST:])
        else:
            if kind == 1:
                b_in = sgu_b_in.reshape(sgu_b_in.shape[0], 1, -1)
                (u,) = _matmul(h, sgu_w_in, j, 0, W, [BF16], act="gelu", bias=b_in)
                (v_raw,) = _matmul(h, sgu_w_in, j, W, W, [F32], act="gelu", bias=b_in)
                gate_args = (vec(sgu_g_v, j), sgu_w_s[j], sgu_b_s[j].T)
                (gated,) = _sgu_gate_segment(u, v_raw, 0, Mp, min(T, SGU_BLOCK), *gate_args, None, False)
                gated, v_s = _sgu_gate_segment(u, v_raw, Mp, Ms, min(Ts, SGU_BLOCK), *gate_args, gated, True)
                sgu_v_s.append(v_s.reshape(Bs, Ts, W))
                (m,) = _matmul(gated, sgu_w_out, j, 0, D, [F32])
            else:
                (q,) = _matmul(h, sb_w_qkv, j, 0, D, [BF16])
                k32, k = _matmul(h, sb_w_qkv, j, D, D, [F32, BF16])
                v32, v = _matmul(h, sb_w_qkv, j, 2 * D, D, [F32, BF16])
                o = _sb_prompt(q, k, v, B, T, H, dh, M)
                o = _sb_sample(q, k, v, cache_k, cache_v, j, Mp, Bs, Ts, H, dh, o)
                (m,) = _matmul(o, sb_w_o, j, 0, D, [F32])
                sb_k_p.append(k32[:Mp].reshape(B, T, H, dh)[:, -keep:])
                sb_v_p.append(v32[:Mp].reshape(B, T, H, dh)[:, -keep:])
                sb_k_s.append(k32[Mp:].reshape(Bs, Ts, H, dh))
                sb_v_s.append(v32[Mp:].reshape(Bs, Ts, H, dh))
            y, h = _resnorm(y, m, vec(g_mix_post, i), g_ffn)
        a, w_down_bf = _matmul(h, ffn_w_up, i, 0, ffn_w_up.shape[2], [BF16], act="relu2",
                               side=(ffn_w_down, i))
        m = _matmul_kgrid(a, w_down_bf, F32)
        g_post = vec(g_ffn_post, i)
        if i + 1 == depth:
            (yp,) = _resnorm(y, m, g_post, None, 0, Mp)
            (ys,) = _resnorm(y, m, g_post, None, Mp, Ms)
        elif (i + 1) % 3 == 0:
            (y,) = _resnorm(y, m, g_post)
        else:
            y, h = _resnorm(y, m, g_post, vec(g_mix_pre, i + 1))
    return (yp.reshape(B, T, D), ys.reshape(Bs, Ts, D),
            jnp.stack(pool_hist_p), jnp.stack(pool_hist_s), jnp.stack(sgu_v_s),
            jnp.stack(sb_k_p), jnp.stack(sb_v_p), jnp.stack(sb_k_s), jnp.stack(sb_v_s))
```
